```python
import math
import jax, jax.numpy as jnp
from jax import lax
import numpy as np

D_MODEL = 1024
BATCH = 4
SEQ = 4096
DEPTH = 1
DEC_BATCH = 32
DEC_SEQ = 4
PAST_LEN = 16384
PAGE_SIZE = 128

N_META = 16
BLOCK = 128
D_FF = 2816
NORM_EPS = 1e-6
SSD_HEADS = 16
SSD_HEAD_DIM = 64
SSD_INNER = SSD_HEADS * SSD_HEAD_DIM
SSD_STATE = 128
SSD_GROUPS = 2
CONV_WIDTH = 4
CONV_DIM = SSD_INNER + 2 * SSD_GROUPS * SSD_STATE
DT_MIN = 1e-3
DT_MAX = 1e-1
FOX_HEADS = 16
FOX_HEAD_DIM = 64
FOX_INNER = FOX_HEADS * FOX_HEAD_DIM
FORGET_BIAS_INIT = 3.0
IN_SIZES = (SSD_INNER, CONV_DIM, SSD_HEADS, FOX_INNER, FOX_INNER, FOX_INNER, FOX_HEADS, D_MODEL, D_MODEL)
IN_COLS = sum(IN_SIZES)
NEG_INF = -1e30

kernel_name = 'fox_ssd_gated_hybrid_step'

F32 = jnp.float32


def rmsnorm(x, w):
    xf = x.astype(F32)
    y = xf * lax.rsqrt(jnp.mean(xf * xf, axis=-1, keepdims=True) + NORM_EPS)
    return (y * w.astype(F32)).astype(x.dtype)


def half_ffn(x, norm_w, w_gate, w_up, w_down):
    hn = rmsnorm(x, norm_w)
    return x + 0.5 * ((jax.nn.silu(hn @ w_gate) * (hn @ w_up)) @ w_down)


def split_cols(u):
    bounds = [int(v) for v in np.cumsum(IN_SIZES)[:-1]]
    return jnp.split(u, bounds, axis=-1)


def split_xbc(xbc):
    gn = SSD_GROUPS * SSD_STATE
    return xbc[..., :SSD_INNER], xbc[..., SSD_INNER:SSD_INNER + gn], xbc[..., SSD_INNER + gn:]


def heads(a, n_heads, head_dim):
    return a.reshape(a.shape[0], a.shape[1], n_heads, head_dim)


def causal_conv(xp, w, b):
    out = lax.conv_general_dilated(xp, w[:, None, :].astype(xp.dtype), (1,), 'VALID',
                                   dimension_numbers=('NWC', 'WIO', 'NWC'),
                                   feature_group_count=xp.shape[-1])
    return out + b


def ssd_chunked(x, dt, A, Bm, Cm):
    b, L = x.shape[0], x.shape[1]
    nc = L // BLOCK
    G, R, P, N = SSD_GROUPS, SSD_HEADS // SSD_GROUPS, SSD_HEAD_DIM, SSD_STATE
    xc = x.astype(F32).reshape(b, nc, BLOCK, G, R, P)
    dtc = dt.reshape(b, nc, BLOCK, G, R)
    Bc = Bm.astype(F32).reshape(b, nc, BLOCK, G, N)
    Cc = Cm.astype(F32).reshape(b, nc, BLOCK, G, N)
    a_cum = jnp.cumsum(dtc * A.reshape(G, R), axis=2)
    causal = jnp.tril(jnp.ones((BLOCK, BLOCK), bool))[:, :, None, None]
    seg = a_cum[:, :, :, None] - a_cum[:, :, None, :]
    decay = jnp.exp(jnp.where(causal, seg, -jnp.inf))
    cb = jnp.einsum('bclgn,bcsgn->bclsg', Cc, Bc)
    w = cb[..., None] * decay * dtc[:, :, None]
    y_diag = jnp.einsum('bclsgr,bcsgrp->bclgrp', w, xc)
    to_end = jnp.exp(a_cum[:, :, -1:] - a_cum) * dtc
    chunk_states = jnp.einsum('bclgn,bclgr,bclgrp->bcgrpn', Bc, to_end, xc)
    chunk_decay = jnp.exp(a_cum[:, :, -1])

    def carry_step(h, inp):
        s, d = inp
        return h * d[..., None, None] + s, h

    h0 = jnp.zeros((b, G, R, P, N), F32)
    h_final, h_in = lax.scan(carry_step, h0, (jnp.moveaxis(chunk_states, 1, 0), jnp.moveaxis(chunk_decay, 1, 0)))
    h_in = jnp.moveaxis(h_in, 0, 1)
    y_off = jnp.einsum('bclgn,bcgrpn,bclgr->bclgrp', Cc, h_in, jnp.exp(a_cum))
    y = (y_diag + y_off).reshape(b, L, SSD_HEADS, P)
    return y, h_final.reshape(b, SSD_HEADS, P, N)


def ssd_recurrent(x, dt, A, Bm, Cm, h0):
    b, T = x.shape[0], x.shape[1]
    G, R, P, N = SSD_GROUPS, SSD_HEADS // SSD_GROUPS, SSD_HEAD_DIM, SSD_STATE
    Ag = A.reshape(G, R)

    def step(h, inp):
        xt, dtt, bt, ct = inp
        h = h * jnp.exp(dtt * Ag)[..., None, None] + jnp.einsum('bgr,bgrp,bgn->bgrpn', dtt, xt, bt)
        return h, jnp.einsum('bgn,bgrpn->bgrp', ct, h)

    seq = (jnp.moveaxis(x.astype(F32).reshape(b, T, G, R, P), 1, 0),
           jnp.moveaxis(dt.reshape(b, T, G, R), 1, 0),
           jnp.moveaxis(Bm.astype(F32), 1, 0),
           jnp.moveaxis(Cm.astype(F32), 1, 0))
    h, ys = lax.scan(step, h0.astype(F32).reshape(b, G, R, P, N), seq)
    return jnp.moveaxis(ys, 0, 1).reshape(b, T, SSD_HEADS, P), h.reshape(b, SSD_HEADS, P, N)


def ssd_out(y, xs, z, D_skip, ssd_norm, w_ssd_out):
    b, T = y.shape[0], y.shape[1]
    y = y + xs.astype(F32).reshape(b, T, SSD_HEADS, SSD_HEAD_DIM) * D_skip.astype(F32)[:, None]
    y = y.reshape(b, T, SSD_INNER) * jax.nn.silu(z.astype(F32))
    return rmsnorm(y, ssd_norm).astype(z.dtype) @ w_ssd_out


def fox_prompt(q, k, v, logf, valid):
    b, L = q.shape[0], q.shape[1]
    nb = L // BLOCK
    scale = FOX_HEAD_DIM ** -0.5
    c = jnp.cumsum(logf, axis=1)
    c_k = jnp.transpose(c, (0, 2, 1))[:, :, None, :]
    kpos = jnp.arange(L)

    def block(i):
        qs = i * BLOCK
        qb = lax.dynamic_slice_in_dim(q, qs, BLOCK, axis=1)
        cq = jnp.transpose(lax.dynamic_slice_in_dim(c, qs, BLOCK, axis=1), (0, 2, 1))[..., None]
        qpos = qs + jnp.arange(BLOCK)
        s = jnp.einsum('bqhd,bkhd->bhqk', qb, k, preferred_element_type=F32) * scale + (cq - c_k)
        mask = (kpos[None, :] <= qpos[:, None]) & valid[None, :]
        p = jax.nn.softmax(jnp.where(mask, s, NEG_INF), axis=-1)
        return jnp.einsum('bhqk,bkhd->bqhd', p.astype(v.dtype), v)

    out = lax.map(block, jnp.arange(nb))
    return jnp.moveaxis(out, 0, 1).reshape(b, L, FOX_HEADS, FOX_HEAD_DIM)


def fox_sample(q, k_new, v_new, logf_new, cache_k, cache_v, cache_logf, page_table):
    scale = FOX_HEAD_DIM ** -0.5

    def one_seq(args):
        qi, kn, vn, lfn, pages = args
        kp = cache_k[pages].reshape(-1, FOX_HEADS, FOX_HEAD_DIM)
        vp = cache_v[pages].reshape(-1, FOX_HEADS, FOX_HEAD_DIM)
        lfp = cache_logf[pages].reshape(-1, FOX_HEADS).astype(F32)
        P, T = kp.shape[0], qi.shape[0]
        c = jnp.cumsum(jnp.concatenate([lfp, lfn], axis=0), axis=0)
        bias = c[P:].T[:, :, None] - c.T[:, None, :]
        s = jnp.concatenate([jnp.einsum('qhd,khd->hqk', qi, kp, preferred_element_type=F32),
                             jnp.einsum('qhd,khd->hqk', qi, kn, preferred_element_type=F32)], axis=-1)
        s = s * scale + bias
        causal = jnp.arange(T)[None, :] <= jnp.arange(T)[:, None]
        mask = jnp.concatenate([jnp.ones((T, P), bool), causal], axis=-1)
        p = jax.nn.softmax(jnp.where(mask, s, NEG_INF), axis=-1).astype(vp.dtype)
        return jnp.einsum('hqk,khd->qhd', p[..., :P], vp) + jnp.einsum('hqk,khd->qhd', p[..., P:], vn)

    return lax.map(one_seq, (q, k_new, v_new, logf_new, page_table))


def merge_branches(o_ssd, o_fox, g_ssd, g_fox, w_o):
    return (jax.nn.sigmoid(g_ssd) * o_ssd + jax.nn.sigmoid(g_fox) * o_fox) @ w_o


def mixer_prompt(h, mw):
    w_in, conv_w, conv_b, dt_bias, A_log, D_skip, ssd_norm, w_ssd_out, b_forget, w_fox_out, w_o = mw
    b, T = h.shape[0], h.shape[1]
    pad = BLOCK - N_META
    L = pad + T
    z, xbc, dt_raw, q, k, v, f_raw, g_ssd, g_fox = split_cols(h @ w_in)
    valid = jnp.arange(L) >= pad
    front = lambda a: jnp.pad(a, ((0, 0), (pad, 0)) + ((0, 0),) * (a.ndim - 2))
    xbc_c = jax.nn.silu(causal_conv(jnp.pad(xbc, ((0, 0), (pad + CONV_WIDTH - 1, 0), (0, 0))), conv_w, conv_b))
    xs, Bm, Cm = split_xbc(xbc_c)
    dt = jnp.where(valid[None, :, None], jax.nn.softplus(front(dt_raw).astype(F32) + dt_bias), 0.0)
    A = -jnp.exp(A_log.astype(F32))
    y, ssm_final = ssd_chunked(heads(xs, SSD_HEADS, SSD_HEAD_DIM), dt, A,
                               heads(Bm, SSD_GROUPS, SSD_STATE), heads(Cm, SSD_GROUPS, SSD_STATE))
    o_ssd = ssd_out(y[:, pad:], xs[:, pad:], z, D_skip, ssd_norm, w_ssd_out)
    conv_state = xbc[:, T - (CONV_WIDTH - 1):]
    logf = jax.nn.log_sigmoid(f_raw.astype(F32) + b_forget)
    qh, kh, vh = (heads(a, FOX_HEADS, FOX_HEAD_DIM) for a in (q, k, v))
    o = fox_prompt(front(qh), front(kh), front(vh), front(logf), valid)[:, pad:]
    o_fox = o.reshape(b, T, FOX_INNER) @ w_fox_out
    out = merge_branches(o_ssd, o_fox, g_ssd, g_fox, w_o)
    return out, (kh, vh, logf, conv_state, ssm_final)


def mixer_sample(h, cache_k, cache_v, cache_logf, page_table, conv_state, ssm_state, mw):
    w_in, conv_w, conv_b, dt_bias, A_log, D_skip, ssd_norm, w_ssd_out, b_forget, w_fox_out, w_o = mw
    b, T = h.shape[0], h.shape[1]
    z, xbc, dt_raw, q, k, v, f_raw, g_ssd, g_fox = split_cols(h @ w_in)
    xbc_full = jnp.concatenate([conv_state.astype(xbc.dtype), xbc], axis=1)
    new_conv = xbc_full[:, T:]
    xbc_c = jax.nn.silu(causal_conv(xbc_full, conv_w, conv_b))
    xs, Bm, Cm = split_xbc(xbc_c)
    dt = jax.nn.softplus(dt_raw.astype(F32) + dt_bias)
    A = -jnp.exp(A_log.astype(F32))
    y, ssm_new = ssd_recurrent(heads(xs, SSD_HEADS, SSD_HEAD_DIM), dt, A,
                               heads(Bm, SSD_GROUPS, SSD_STATE), heads(Cm, SSD_GROUPS, SSD_STATE), ssm_state)
    o_ssd = ssd_out(y, xs, z, D_skip, ssd_norm, w_ssd_out)
    logf = jax.nn.log_sigmoid(f_raw.astype(F32) + b_forget)
    qh, kh, vh = (heads(a, FOX_HEADS, FOX_HEAD_DIM) for a in (q, k, v))
    o = fox_sample(qh, kh, vh, logf, cache_k, cache_v, cache_logf, page_table)
    o_fox = o.reshape(b, T, FOX_INNER) @ w_fox_out
    out = merge_branches(o_ssd, o_fox, g_ssd, g_fox, w_o)
    return out, (kh, vh, logf, new_conv, ssm_new)


def setup_inputs(seed: int = 0) -> dict:
    key = jax.random.key(seed)
    ks = list(jax.random.split(key, 40))
    nrm = lambda shape, scale: jax.random.normal(ks.pop(), shape, F32) * scale
    n_pages = PAST_LEN // PAGE_SIZE
    n_phys = (DEC_BATCH * n_pages * 5) // 4
    Lr = DEPTH
    gain = lambda shape: 1.0 + nrm(shape, 0.05)
    x_prompt = nrm((BATCH, SEQ, D_MODEL), 1.0)
    x_sample = nrm((DEC_BATCH, DEC_SEQ, D_MODEL), 1.0)
    cache_k = nrm((Lr, n_phys, PAGE_SIZE, FOX_HEADS, FOX_HEAD_DIM), 1.0)
    cache_v = nrm((Lr, n_phys, PAGE_SIZE, FOX_HEADS, FOX_HEAD_DIM), 1.0)
    cache_logf = jax.nn.log_sigmoid(FORGET_BIAS_INIT + nrm((Lr, n_phys, PAGE_SIZE, FOX_HEADS), 1.0))
    page_table = jax.random.permutation(ks.pop(), n_phys)[:DEC_BATCH * n_pages].reshape(DEC_BATCH, n_pages).astype(jnp.int32)
    state_conv = nrm((Lr, DEC_BATCH, CONV_WIDTH - 1, CONV_DIM), 1.0)
    state_ssm = nrm((Lr, DEC_BATCH, SSD_HEADS, SSD_HEAD_DIM, SSD_STATE), 0.1)
    meta_tokens = nrm((N_META, D_MODEL), 1.0)
    dt0 = jnp.exp(jax.random.uniform(ks.pop(), (Lr, SSD_HEADS), F32, math.log(DT_MIN), math.log(DT_MAX)))
    dt_bias = dt0 + jnp.log(-jnp.expm1(-dt0))
    A_log = jnp.log(jax.random.uniform(ks.pop(), (Lr, SSD_HEADS), F32, 1.0, 16.0))
    return {
        'x_prompt': x_prompt,
        'x_sample': x_sample,
        'cache_k': cache_k,
        'cache_v': cache_v,
        'cache_logf': cache_logf,
        'page_table': page_table,
        'state_conv': state_conv,
        'state_ssm': state_ssm,
        'meta_tokens': meta_tokens,
        'norm_ffn1': gain((Lr, D_MODEL)),
        'w_ffn1_gate': nrm((Lr, D_MODEL, D_FF), D_MODEL ** -0.5),
        'w_ffn1_up': nrm((Lr, D_MODEL, D_FF), D_MODEL ** -0.5),
        'w_ffn1_down': nrm((Lr, D_FF, D_MODEL), D_FF ** -0.5),
        'norm_mix': gain((Lr, D_MODEL)),
        'w_in': nrm((Lr, D_MODEL, IN_COLS), D_MODEL ** -0.5),
        'conv_w': nrm((Lr, CONV_WIDTH, CONV_DIM), CONV_WIDTH ** -0.5),
        'conv_b': nrm((Lr, CONV_DIM), 0.01),
        'dt_bias': dt_bias,
        'A_log': A_log,
        'D_skip': gain((Lr, SSD_HEADS)),
        'ssd_norm': gain((Lr, SSD_INNER)),
        'w_ssd_out': nrm((Lr, SSD_INNER, D_MODEL), SSD_INNER ** -0.5),
        'b_forget': FORGET_BIAS_INIT + nrm((Lr, FOX_HEADS), 0.1),
        'w_fox_out': nrm((Lr, FOX_INNER, D_MODEL), FOX_INNER ** -0.5),
        'w_o': nrm((Lr, D_MODEL, D_MODEL), D_MODEL ** -0.5),
        'norm_ffn2': gain((Lr, D_MODEL)),
        'w_ffn2_gate': nrm((Lr, D_MODEL, D_FF), D_MODEL ** -0.5),
        'w_ffn2_up': nrm((Lr, D_MODEL, D_FF), D_MODEL ** -0.5),
        'w_ffn2_down': nrm((Lr, D_FF, D_MODEL), D_FF ** -0.5),
        'norm_final': gain((D_MODEL,)),
    }


def reference(x_prompt, x_sample, cache_k, cache_v, cache_logf, page_table, state_conv, state_ssm,
              meta_tokens, norm_ffn1, w_ffn1_gate, w_ffn1_up, w_ffn1_down, norm_mix, w_in, conv_w, conv_b,
              dt_bias, A_log, D_skip, ssd_norm, w_ssd_out, b_forget, w_fox_out, w_o,
              norm_ffn2, w_ffn2_gate, w_ffn2_up, w_ffn2_down, norm_final):
    b = x_prompt.shape[0]
    meta = jnp.broadcast_to(meta_tokens[None].astype(x_prompt.dtype), (b, N_META, D_MODEL))
    xp = jnp.concatenate([meta, x_prompt], axis=1)
    xs = x_sample
    new_p = []
    new_s = []
    for l in range(DEPTH):
        mw = (w_in[l], conv_w[l], conv_b[l], dt_bias[l], A_log[l], D_skip[l], ssd_norm[l],
              w_ssd_out[l], b_forget[l], w_fox_out[l], w_o[l])
        xp = half_ffn(xp, norm_ffn1[l], w_ffn1_gate[l], w_ffn1_up[l], w_ffn1_down[l])
        xs = half_ffn(xs, norm_ffn1[l], w_ffn1_gate[l], w_ffn1_up[l], w_ffn1_down[l])
        m_p, st_p = mixer_prompt(rmsnorm(xp, norm_mix[l]), mw)
        m_s, st_s = mixer_sample(rmsnorm(xs, norm_mix[l]), cache_k[l], cache_v[l], cache_logf[l],
                                 page_table, state_conv[l], state_ssm[l], mw)
        xp = xp + m_p
        xs = xs + m_s
        xp = half_ffn(xp, norm_ffn2[l], w_ffn2_gate[l], w_ffn2_up[l], w_ffn2_down[l])
        xs = half_ffn(xs, norm_ffn2[l], w_ffn2_gate[l], w_ffn2_up[l], w_ffn2_down[l])
        new_p.append(st_p)
        new_s.append(st_s)
    y_prompt = rmsnorm(xp, norm_final)[:, N_META:]
    y_sample = rmsnorm(xs, norm_final)
    k_p, v_p, lf_p, conv_p, ssm_p = [jnp.stack(a) for a in zip(*new_p)]
    k_s, v_s, lf_s, conv_s, ssm_s = [jnp.stack(a) for a in zip(*new_s)]
    return (y_prompt, y_sample, k_p, v_p, lf_p, conv_p, ssm_p, k_s, v_s, lf_s, conv_s, ssm_s)
```

```python
import functools

import jax
import jax.numpy as jnp
from jax import lax
from jax.experimental import pallas as pl
from jax.experimental.pallas import tpu as pltpu

F32 = jnp.float32
BF16 = jnp.bfloat16

D_MODEL = 1024
D_FF = 2816
NORM_EPS = 1e-6
N_META = 16
BLOCK = 128
PAD = BLOCK - N_META
SSD_HEADS = 16
SSD_HEAD_DIM = 64
SSD_INNER = SSD_HEADS * SSD_HEAD_DIM
SSD_STATE = 128
SSD_GROUPS = 2
GROUP_W = SSD_INNER // SSD_GROUPS
CONV_WIDTH = 4
CONV_DIM = SSD_INNER + 2 * SSD_GROUPS * SSD_STATE
FOX_HEADS = 16
FOX_HEAD_DIM = 64
FOX_INNER = FOX_HEADS * FOX_HEAD_DIM
FOX_SCALE = FOX_HEAD_DIM ** -0.5
NEG_INF = -1e30

LANES = 128
SUBLANES = 8
V7X_VMEM_BYTES = 64 * 1024 * 1024

FF_CHUNK = 256
HIST = SUBLANES


def _dot(a, b):
    return jnp.dot(a, b, preferred_element_type=F32)


def _dot_nt(a, b):
    return lax.dot_general(a, b, (((1,), (1,)), ((), ())), preferred_element_type=F32)


def _split3(x):
    hi = x.astype(BF16)
    r = x - hi.astype(F32)
    mid = r.astype(BF16)
    lo = (r - mid.astype(F32)).astype(BF16)
    return hi, mid, lo


def _dot_exact(a01, x):
    hi, mid, lo = _split3(x)
    return _dot(a01, hi) + _dot(a01, mid) + _dot(a01, lo)


def _dot_exact_r(x, b01):
    hi, mid, lo = _split3(x)
    return _dot(hi, b01) + _dot(mid, b01) + _dot(lo, b01)


def _transpose_exact(x):
    n = x.shape[0]
    eye = (lax.broadcasted_iota(jnp.int32, (n, n), 0) == lax.broadcasted_iota(jnp.int32, (n, n), 1)).astype(BF16)
    hi, mid, lo = _split3(x)
    return _dot_nt(eye, hi) + _dot_nt(eye, mid) + _dot_nt(eye, lo)


def _tril(n):
    return (lax.broadcasted_iota(jnp.int32, (n, n), 1) <= lax.broadcasted_iota(jnp.int32, (n, n), 0)).astype(BF16)


def _triu(n):
    return (lax.broadcasted_iota(jnp.int32, (n, n), 0) <= lax.broadcasted_iota(jnp.int32, (n, n), 1)).astype(BF16)


def _rms(x):
    return x * lax.rsqrt(jnp.mean(x * x, axis=-1, keepdims=True) + NORM_EPS)


def _silu(x):
    return x * jax.nn.sigmoid(x)


def _params(n_axes, vmem_mib):
    return pltpu.CompilerParams(dimension_semantics=("arbitrary",) * n_axes,
                                vmem_limit_bytes=min(vmem_mib * 1024 * 1024, V7X_VMEM_BYTES - 4 * 1024 * 1024))


def _resident(shape):
    nd = len(shape)
    return pl.BlockSpec(shape, lambda *_: (0,) * nd, pipeline_mode=pl.Buffered(1))


def _ffn_body(*refs, merge, final_norm):
    refs = list(refs)
    x_ref = refs.pop(0)
    if merge:
        os_ref, oa_ref, gs_ref, gf_ref, wf_ref, wo_ref = refs[:6]
        refs = refs[6:]
    nw_ref, wg_ref, wu_ref, wd_ref = refs[:4]
    refs = refs[4:]
    if final_norm:
        fnw_ref = refs.pop(0)
    o_ref, hn_ref, acc_ref = refs

    x = x_ref[...]
    if merge:
        o_fox = _dot(oa_ref[...].astype(BF16), wf_ref[...])
        mix = jax.nn.sigmoid(gs_ref[...]) * os_ref[...] + jax.nn.sigmoid(gf_ref[...]) * o_fox
        x = x + _dot(mix.astype(BF16), wo_ref[...])
    hn_ref[...] = (_rms(x) * nw_ref[...]).astype(BF16)
    acc_ref[...] = x

    def chunk(c, carry):
        hn = hn_ref[...]
        g = _dot(hn, wg_ref[c])
        u = _dot(hn, wu_ref[c])
        a = (_silu(g) * u).astype(BF16)
        acc_ref[...] += 0.5 * _dot(a, wd_ref[c])
        return carry

    lax.fori_loop(0, wg_ref.shape[0], chunk, 0)
    y = acc_ref[...]
    if final_norm:
        y = _rms(y) * fnw_ref[...]
    o_ref[...] = y


def _ffn(x, n_rows, tm, norm_w, wg3, wu3, wd3, merge_args=None, final_norm_w=None, name="ffn"):
    merge = merge_args is not None
    row_spec = pl.BlockSpec((tm, D_MODEL), lambda i: (i, 0))
    args, specs = [x], [row_spec]
    if merge:
        o_ssd, o_att, g_ssd, g_fox, w_fox, w_o = merge_args
        args += [o_ssd, o_att, g_ssd, g_fox, w_fox, w_o]
        specs += [row_spec] * 4 + [_resident(w_fox.shape), _resident(w_o.shape)]
    args += [norm_w, wg3, wu3, wd3]
    specs += [_resident(norm_w.shape), _resident(wg3.shape), _resident(wu3.shape), _resident(wd3.shape)]
    if final_norm_w is not None:
        args.append(final_norm_w)
        specs.append(_resident(final_norm_w.shape))
    return pl.pallas_call(
        functools.partial(_ffn_body, merge=merge, final_norm=final_norm_w is not None),
        grid=(n_rows // tm,),
        in_specs=specs,
        out_specs=row_spec,
        out_shape=jax.ShapeDtypeStruct((n_rows, D_MODEL), F32),
        scratch_shapes=[pltpu.VMEM((tm, D_MODEL), BF16), pltpu.VMEM((tm, D_MODEL), F32)],
        compiler_params=_params(1, 56),
        name=name,
    )(*args)


_MAIN_WIDTHS = (SSD_INNER, CONV_DIM, FOX_INNER, FOX_INNER, FOX_INNER, D_MODEL, D_MODEL)


def _inproj_body(x_ref, nw_ref, wm_ref, ws_ref, bias_ref, *refs):
    outs, sm_ref, hn_ref = refs[:7], refs[7], refs[8]
    hn_ref[...] = (_rms(x_ref[...]) * nw_ref[...]).astype(BF16)
    off = 0
    for ref, w in zip(outs, _MAIN_WIDTHS):
        ref[...] = _dot(hn_ref[...], wm_ref[:, off:off + w])
        off += w
    raw = _dot(hn_ref[...], ws_ref[...]) + bias_ref[...]
    t = jnp.log1p(jnp.exp(-jnp.abs(raw)))
    lane = lax.broadcasted_iota(jnp.int32, raw.shape, 1)
    sm_ref[...] = jnp.where(lane < SSD_HEADS, jnp.maximum(raw, 0.0) + t,
                            jnp.where(lane < SSD_HEADS + FOX_HEADS, jnp.minimum(raw, 0.0) - t, 0.0))


def _inproj(x, tm, norm_w, w_main, w_small, bias_small, name):
    rows = x.shape[0]
    row = lambda w: pl.BlockSpec((tm, w), lambda i: (i, 0))
    widths = _MAIN_WIDTHS + (LANES,)
    return pl.pallas_call(
        _inproj_body,
        grid=(rows // tm,),
        in_specs=[row(D_MODEL), _resident(norm_w.shape), _resident(w_main.shape), _resident(w_small.shape),
                  _resident(bias_small.shape)],
        out_specs=[row(w) for w in widths],
        out_shape=[jax.ShapeDtypeStruct((rows, w), F32) for w in widths],
        scratch_shapes=[pltpu.VMEM((tm, D_MODEL), BF16)],
        compiler_params=_params(1, 56),
        name=name,
    )(x, norm_w, w_main, w_small, bias_small)


def _cumsum_rows(x):
    return _dot_exact(_tril(x.shape[0]), x)


def _logf_cumsum_body(lt_ref, lm_ref, ct_ref, cm_ref, carry_ref):
    @pl.when(pl.program_id(1) == 0)
    def _():
        row = lax.broadcasted_iota(jnp.int32, (BLOCK, LANES), 0)
        cm = _cumsum_rows(jnp.where(row >= PAD, lm_ref[...], 0.0))
        cm_ref[...] = cm
        carry_ref[...] = cm[BLOCK - 1:BLOCK, :]

    c = _cumsum_rows(lt_ref[...]) + carry_ref[...]
    ct_ref[...] = c
    carry_ref[...] = c[c.shape[0] - 1:, :]


def _logf_cumsum(sm_tok, sm_small, meta_blk, batch, seq, tb):
    nt = seq // tb
    return pl.pallas_call(
        _logf_cumsum_body,
        grid=(batch, nt),
        in_specs=[pl.BlockSpec((tb, LANES), lambda b, j: (b * nt + j, 0)),
                  pl.BlockSpec((BLOCK, LANES), lambda b, j: (meta_blk, 0))],
        out_specs=[pl.BlockSpec((tb, LANES), lambda b, j: (b * nt + j, 0)),
                   pl.BlockSpec((None, BLOCK, LANES), lambda b, j: (b, 0, 0))],
        out_shape=[jax.ShapeDtypeStruct((batch * seq, LANES), F32),
                   jax.ShapeDtypeStruct((batch, BLOCK, LANES), F32)],
        scratch_shapes=[pltpu.VMEM((1, LANES), F32)],
        compiler_params=_params(2, 32),
        name="logf_cumsum",
    )(sm_tok, sm_small)


def _fox_prompt_body(q_ref, kt_ref, vt_ref, km_ref, vm_ref, ct_ref, cm_ref, o_ref,
                     k_sc, v0_sc, v1_sc, km_sc, vm0_sc, vm1_sc, *, tq):
    qi = pl.program_id(2)
    is0 = lax.broadcasted_iota(jnp.int32, (1, LANES), 1) < FOX_HEAD_DIM

    @pl.when(qi == 0)
    def _():
        km_sc[...] = km_ref[...].astype(BF16)
        vm = vm_ref[...]
        vm0_sc[...] = jnp.where(is0, vm, 0.0).astype(BF16)
        vm1_sc[...] = jnp.where(is0, 0.0, vm).astype(BF16)

        def cast_block(i, carry):
            r = pl.multiple_of(i * tq, tq)
            k_sc[i] = kt_ref[pl.ds(r, tq), :].astype(BF16)
            vv = vt_ref[pl.ds(r, tq), :]
            v0_sc[i] = jnp.where(is0, vv, 0.0).astype(BF16)
            v1_sc[i] = jnp.where(is0, 0.0, vv).astype(BF16)
            return carry

        lax.fori_loop(0, k_sc.shape[0], cast_block, 0)

    q = q_ref[...] * FOX_SCALE
    q0 = jnp.where(is0, q, 0.0).astype(BF16)
    q1 = jnp.where(is0, 0.0, q).astype(BF16)

    def update(carry, kb, v0b, v1b, cb, mask):
        m0, l0, m1, l1, acc = carry
        s0 = _dot_nt(q0, kb) - cb[0:1, :]
        s1 = _dot_nt(q1, kb) - cb[1:2, :]
        if mask is not None:
            s0 = jnp.where(mask, s0, NEG_INF)
            s1 = jnp.where(mask, s1, NEG_INF)
        n0 = jnp.maximum(m0, jnp.max(s0, axis=1, keepdims=True))
        n1 = jnp.maximum(m1, jnp.max(s1, axis=1, keepdims=True))
        a0 = jnp.exp(m0 - n0)
        a1 = jnp.exp(m1 - n1)
        p0 = jnp.exp(s0 - n0)
        p1 = jnp.exp(s1 - n1)
        l0 = a0 * l0 + jnp.sum(p0, axis=1, keepdims=True)
        l1 = a1 * l1 + jnp.sum(p1, axis=1, keepdims=True)
        acc = acc * jnp.where(is0, a0, a1) + _dot(p0.astype(BF16), v0b) + _dot(p1.astype(BF16), v1b)
        return n0, l0, n1, l1, acc

    neg = jnp.full((tq, 1), NEG_INF, F32)
    zero = jnp.zeros((tq, 1), F32)
    carry = (neg, zero, neg, zero, jnp.zeros((tq, LANES), F32))
    meta_valid = lax.broadcasted_iota(jnp.int32, (1, BLOCK), 1) >= PAD
    carry = update(carry, km_sc[...], vm0_sc[...], vm1_sc[...], cm_ref[...], meta_valid)
    carry = lax.fori_loop(0, qi, lambda j, c: update(c, k_sc[j], v0_sc[j], v1_sc[j], ct_ref[j], None), carry)
    causal = lax.broadcasted_iota(jnp.int32, (1, tq), 1) <= lax.broadcasted_iota(jnp.int32, (tq, 1), 0)
    _, l0, _, l1, acc = update(carry, k_sc[qi], v0_sc[qi], v1_sc[qi], ct_ref[qi], causal)
    o_ref[...] = acc / jnp.where(is0, l0, l1)


def _fox_prompt(q, k, v, k_small, v_small, c_tok, c_meta, meta_blk, batch, seq, tq):
    nq = seq // tq
    npair = FOX_INNER // LANES
    kv_spec = pl.BlockSpec((seq, LANES), lambda b, hp, i: (b, hp))
    meta_spec = pl.BlockSpec((BLOCK, LANES), lambda b, hp, i: (meta_blk, hp))
    blk_spec = pl.BlockSpec((tq, LANES), lambda b, hp, i: (b * nq + i, hp))
    return pl.pallas_call(
        functools.partial(_fox_prompt_body, tq=tq),
        grid=(batch, npair, nq),
        in_specs=[blk_spec, kv_spec, kv_spec, meta_spec, meta_spec,
                  pl.BlockSpec((None, None, nq, 2, tq), lambda b, hp, i: (b, hp, 0, 0, 0)),
                  pl.BlockSpec((None, None, 2, BLOCK), lambda b, hp, i: (b, hp, 0, 0))],
        out_specs=blk_spec,
        out_shape=jax.ShapeDtypeStruct((batch * seq, FOX_INNER), F32),
        scratch_shapes=[pltpu.VMEM((nq, tq, LANES), BF16)] * 3 + [pltpu.VMEM((BLOCK, LANES), BF16)] * 3,
        compiler_params=_params(3, 40),
        name="fox_prompt",
    )(q, k, v, k_small, v_small, c_tok, c_meta)


def _ssd_chunk(xbuf_ref, dt, z, st_ref, ybuf_ref, cw_ref, cb_ref, a_ref, dskip_ref, nw_ref, wout_ref):
    L = BLOCK
    conv = cb_ref[...]
    for w in range(CONV_WIDTH):
        conv = conv + xbuf_ref[HIST - (CONV_WIDTH - 1) + w:HIST - (CONV_WIDTH - 1) + w + L, :] * cw_ref[w:w + 1, :]
    xc = _silu(conv)
    xs = xc[:, :SSD_INNER]
    gn = SSD_GROUPS * SSD_STATE

    a_col = _cumsum_rows(dt * a_ref[...])
    a_row = _transpose_exact(a_col)
    dt_row = _transpose_exact(dt)
    lane_head = lax.broadcasted_iota(jnp.int32, (LANES, SSD_HEADS * LANES), 1) // LANES
    e_wide = (lane_head == lax.broadcasted_iota(jnp.int32, (LANES, SSD_HEADS * LANES), 0)).astype(BF16)
    a_bc = _dot_exact_r(a_col, e_wide)
    col_head = lax.broadcasted_iota(jnp.int32, (LANES, SSD_INNER), 1) // SSD_HEAD_DIM
    e_head = (col_head == lax.broadcasted_iota(jnp.int32, (LANES, SSD_INNER), 0)).astype(BF16)
    a_exp = _dot_exact_r(a_col, e_head)
    decay_in = jnp.exp(a_exp)
    a_last = a_exp[L - 1:L, :]
    to_end = jnp.exp(a_last - a_exp) * _dot_exact_r(dt, e_head)
    xw = (xs * to_end).astype(BF16)
    chunk_decay = decay_in[L - 1:L, :]

    causal = lax.broadcasted_iota(jnp.int32, (L, L), 1) <= lax.broadcasted_iota(jnp.int32, (L, L), 0)
    is0 = lax.broadcasted_iota(jnp.int32, (1, LANES), 1) < SSD_HEAD_DIM
    heads_per_group = SSD_HEADS // SSD_GROUPS
    for g in range(SSD_GROUPS):
        b_g = xc[:, SSD_INNER + g * SSD_STATE:SSD_INNER + (g + 1) * SSD_STATE].astype(BF16)
        c_g = xc[:, SSD_INNER + gn + g * SSD_STATE:SSD_INNER + gn + (g + 1) * SSD_STATE].astype(BF16)
        cb = _dot_nt(c_g, b_g)
        lo, hi = g * GROUP_W, (g + 1) * GROUP_W
        h_in = st_ref[g]
        y_off = _dot(c_g, h_in.astype(BF16)) * decay_in[:, lo:hi]
        ybuf_ref[:, lo:hi] = y_off
        eye = (lax.broadcasted_iota(jnp.int32, (L, L), 0) == lax.broadcasted_iota(jnp.int32, (L, L), 1)).astype(BF16)
        b_t = _dot_nt(eye, b_g).astype(BF16)
        st_ref[g] = h_in * chunk_decay[:, lo:hi] + _dot(b_t, xw[:, lo:hi])
        for pair in range(heads_per_group // 2):
            h0 = g * heads_per_group + 2 * pair
            c0 = h0 * SSD_HEAD_DIM
            x_pair = xs[:, c0:c0 + LANES]
            y_pair = jnp.zeros((L, LANES), F32)
            for k, xm in ((0, jnp.where(is0, x_pair, 0.0)), (1, jnp.where(is0, 0.0, x_pair))):
                h = h0 + k
                seg = a_bc[:, h * LANES:(h + 1) * LANES] - a_row[h:h + 1, :]
                wgt = cb * jnp.exp(jnp.where(causal, seg, -jnp.inf)) * dt_row[h:h + 1, :]
                y_pair = y_pair + _dot(wgt.astype(BF16), xm.astype(BF16))
            ybuf_ref[:, c0:c0 + LANES] += y_pair

    y = (ybuf_ref[...] + xs * dskip_ref[...]) * _silu(z)
    yn = (_rms(y) * nw_ref[...]).astype(BF16)
    return _dot(yn, wout_ref[...])


_SSD_WEIGHT_SHAPES = ((SUBLANES, CONV_DIM), (1, CONV_DIM), (1, LANES), (1, SSD_INNER), (1, SSD_INNER),
                      (SSD_INNER, D_MODEL))


def _ssd_prompt_body(xt_ref, zt_ref, st_ref_in, xm_ref, sm_ref, cw_ref, cb_ref, a_ref, dskip_ref, nw_ref, wout_ref,
                     o_ref, state_ref, xbuf_ref, st_ref, ybuf_ref):
    c = pl.program_id(1)
    lane = lax.broadcasted_iota(jnp.int32, (BLOCK, LANES), 1)
    row = lax.broadcasted_iota(jnp.int32, (BLOCK, LANES), 0)

    @pl.when(c == 0)
    def _():
        st_ref[...] = jnp.zeros_like(st_ref)
        xbuf_ref[0:HIST, :] = jnp.zeros((HIST, CONV_DIM), F32)
        xbuf_ref[HIST:, :] = xm_ref[...]

    @pl.when(c > 0)
    def _():
        xbuf_ref[HIST:, :] = xt_ref[...]

    dt = jnp.where(c == 0,
                   jnp.where((lane < SSD_HEADS) & (row >= PAD), sm_ref[...], 0.0),
                   jnp.where(lane < SSD_HEADS, st_ref_in[...], 0.0))
    o_ref[...] = _ssd_chunk(xbuf_ref, dt, zt_ref[...], st_ref, ybuf_ref, cw_ref, cb_ref, a_ref, dskip_ref, nw_ref,
                            wout_ref)
    xbuf_ref[0:HIST, :] = xbuf_ref[BLOCK:BLOCK + HIST, :]

    @pl.when(c == pl.num_programs(1) - 1)
    def _():
        state_ref[...] = st_ref[...]


def _ssd_prompt(xbc, z, sm, xbc_small, sm_small, meta_blk, weights, batch, seq):
    nc = seq // BLOCK
    tok = lambda w: pl.BlockSpec((BLOCK, w), lambda b, c: (b * nc + jnp.maximum(c - 1, 0), 0))
    meta = lambda w: pl.BlockSpec((BLOCK, w), lambda b, c: (meta_blk, 0))
    return pl.pallas_call(
        _ssd_prompt_body,
        grid=(batch, nc + 1),
        in_specs=[tok(CONV_DIM), tok(SSD_INNER), tok(LANES), meta(CONV_DIM), meta(LANES)]
                 + [_resident(s) for s in _SSD_WEIGHT_SHAPES],
        out_specs=[tok(D_MODEL),
                   pl.BlockSpec((None, SSD_GROUPS, SSD_STATE, GROUP_W), lambda b, c: (b, 0, 0, 0))],
        out_shape=[jax.ShapeDtypeStruct((batch * seq, D_MODEL), F32),
                   jax.ShapeDtypeStruct((batch, SSD_GROUPS, SSD_STATE, GROUP_W), F32)],
        scratch_shapes=[pltpu.VMEM((HIST + BLOCK, CONV_DIM), F32),
                        pltpu.VMEM((SSD_GROUPS, SSD_STATE, GROUP_W), F32),
                        pltpu.VMEM((BLOCK, SSD_INNER), F32)],
        compiler_params=_params(2, 40),
        name="ssd_prompt",
    )(xbc, z, sm, xbc_small, sm_small, *weights)


def _ssd_sample_body(x_ref, z_ref, sm_ref, hist_ref, state0_ref, cw_ref, cb_ref, a_ref, dskip_ref, nw_ref, wout_ref,
                     o_ref, state_ref, xbuf_ref, st_ref, ybuf_ref, dt_ref, zbuf_ref, *, n_new):
    rows = x_ref.shape[0]
    xbuf_ref[0:HIST, :] = hist_ref[...]
    xbuf_ref[HIST:HIST + rows, :] = x_ref[...]
    xbuf_ref[HIST + rows:, :] = jnp.zeros((BLOCK - rows, CONV_DIM), F32)
    lane = lax.broadcasted_iota(jnp.int32, (rows, LANES), 1)
    row = lax.broadcasted_iota(jnp.int32, (rows, LANES), 0)
    dt_ref[...] = jnp.zeros_like(dt_ref)
    dt_ref[0:rows, :] = jnp.where((lane < SSD_HEADS) & (row < n_new), sm_ref[...], 0.0)
    zbuf_ref[...] = jnp.zeros_like(zbuf_ref)
    zbuf_ref[0:rows, :] = z_ref[...]
    st_ref[...] = state0_ref[...]
    out = _ssd_chunk(xbuf_ref, dt_ref[...], zbuf_ref[...], st_ref, ybuf_ref, cw_ref, cb_ref, a_ref, dskip_ref, nw_ref,
                     wout_ref)
    o_ref[...] = out[0:rows, :]
    state_ref[...] = st_ref[...]


def _ssd_sample(x8, z8, sm8, hist8, state0, weights, n_new):
    nb, rows = x8.shape[0], x8.shape[1]
    per = lambda w: pl.BlockSpec((None, rows, w), lambda b: (b, 0, 0))
    st_spec = pl.BlockSpec((None, SSD_GROUPS, SSD_STATE, GROUP_W), lambda b: (b, 0, 0, 0))
    return pl.pallas_call(
        functools.partial(_ssd_sample_body, n_new=n_new),
        grid=(nb,),
        in_specs=[per(CONV_DIM), per(SSD_INNER), per(LANES), per(CONV_DIM), st_spec]
                 + [_resident(s) for s in _SSD_WEIGHT_SHAPES],
        out_specs=[per(D_MODEL), st_spec],
        out_shape=[jax.ShapeDtypeStruct((nb, rows, D_MODEL), F32),
                   jax.ShapeDtypeStruct((nb, SSD_GROUPS, SSD_STATE, GROUP_W), F32)],
        scratch_shapes=[pltpu.VMEM((HIST + BLOCK, CONV_DIM), F32),
                        pltpu.VMEM((SSD_GROUPS, SSD_STATE, GROUP_W), F32),
                        pltpu.VMEM((BLOCK, SSD_INNER), F32),
                        pltpu.VMEM((BLOCK, LANES), F32),
                        pltpu.VMEM((BLOCK, SSD_INNER), F32)],
        compiler_params=_params(1, 40),
        name="ssd_sample",
    )(x8, z8, sm8, hist8, state0, *weights)


def _page_cumsum_body(x_ref, c_ref, tot_ref):
    x = x_ref[...]
    c_ref[...] = _dot_exact_r(x, _triu(LANES))
    tot_ref[...] = _dot_exact_r(x, jnp.ones((LANES, LANES), BF16))


def _page_cumsum(lf_t, tr):
    rows = lf_t.shape[0]
    spec = pl.BlockSpec((tr, LANES), lambda i: (i, 0))
    return pl.pallas_call(
        _page_cumsum_body,
        grid=(rows // tr,),
        in_specs=[spec],
        out_specs=[spec, spec],
        out_shape=[jax.ShapeDtypeStruct((rows, LANES), F32)] * 2,
        compiler_params=_params(1, 32),
        name="page_cumsum",
    )(lf_t)


def _fox_sample_body(pt_ref, q_ref, kn_ref, vn_ref, ln_ref, *refs, pps, n_new):
    k_refs, v_refs, c_refs, t_refs = refs[:pps], refs[pps:2 * pps], refs[2 * pps:3 * pps], refs[3 * pps:4 * pps]
    o_ref, qbd_sc, m_sc, l_sc, acc_sc, carry_sc, kn_sc, vn_sc = refs[4 * pps:]
    del pt_ref
    j = pl.program_id(1)
    nrow = n_new * FOX_HEADS

    @pl.when(j == 0)
    def _():
        col_head = lax.broadcasted_iota(jnp.int32, (FOX_HEADS, FOX_INNER), 1) // FOX_HEAD_DIM
        own = col_head == lax.broadcasted_iota(jnp.int32, (FOX_HEADS, FOX_INNER), 0)
        q = q_ref[...] * FOX_SCALE
        for t in range(n_new):
            qbd_sc[t * FOX_HEADS:(t + 1) * FOX_HEADS, :] = jnp.where(own, q[t:t + 1, :], 0.0).astype(BF16)
        m_sc[...] = jnp.full_like(m_sc, NEG_INF)
        l_sc[...] = jnp.zeros_like(l_sc)
        acc_sc[...] = jnp.zeros_like(acc_sc)
        carry_sc[...] = jnp.zeros_like(carry_sc)

    qbd = qbd_sc[...]

    def scores(k_bf16, bias16):
        return _dot_nt(qbd, k_bf16) - jnp.concatenate([bias16] * n_new, axis=0)

    def accumulate(s_list, v_list):
        m_old = m_sc[...]
        m_new = m_old
        for s in s_list:
            m_new = jnp.maximum(m_new, jnp.max(s, axis=1, keepdims=True))
        alpha = jnp.exp(m_old - m_new)
        l_new = alpha * l_sc[...]
        pv = jnp.zeros((nrow, FOX_INNER), F32)
        for s, v in zip(s_list, v_list):
            p = jnp.exp(s - m_new)
            l_new = l_new + jnp.sum(p, axis=1, keepdims=True)
            pv = pv + _dot(p.astype(BF16), v)
        m_sc[...] = m_new
        l_sc[...] = l_new
        acc_sc[...] = acc_sc[...] * alpha + pv

    carry = carry_sc[...]
    s_list, v_list = [], []
    for i in range(pps):
        s_list.append(scores(k_refs[i][...].astype(BF16), carry + c_refs[i][...]))
        v_list.append(v_refs[i][...].astype(BF16))
        carry = carry + t_refs[i][...]
    carry_sc[...] = carry
    accumulate(s_list, v_list)

    @pl.when(j == pl.num_programs(1) - 1)
    def _():
        rows = kn_ref.shape[0]
        kn_sc[...] = jnp.zeros_like(kn_sc)
        vn_sc[...] = jnp.zeros_like(vn_sc)
        kn_sc[0:rows, :] = kn_ref[...].astype(BF16)
        vn_sc[0:rows, :] = vn_ref[...].astype(BF16)
        bias = carry_sc[...] + _dot_exact_r(ln_ref[...], _triu(LANES))
        s = scores(kn_sc[...], bias)
        key = lax.broadcasted_iota(jnp.int32, (nrow, BLOCK), 1)
        tok = lax.broadcasted_iota(jnp.int32, (nrow, BLOCK), 0) // FOX_HEADS
        s = jnp.where(key <= tok, s, NEG_INF)
        accumulate([s], [vn_sc[...]])
        o = acc_sc[...] / l_sc[...]
        row_head = lax.broadcasted_iota(jnp.int32, (nrow, FOX_INNER), 0) % FOX_HEADS
        col_head = lax.broadcasted_iota(jnp.int32, (nrow, FOX_INNER), 1) // FOX_HEAD_DIM
        o = jnp.where(row_head == col_head, o, 0.0)
        o_ref[...] = jnp.sum(o.reshape(n_new, FOX_HEADS, FOX_INNER), axis=1)


def _fox_sample(page_table, q, k_new8, v_new8, lf_new_t, cache_k, cache_v, c_page, tot_page, pps):
    nb, n_new = q.shape[0], q.shape[1]
    n_pages = page_table.shape[1]
    page = cache_k.shape[1]
    rows8 = k_new8.shape[1]
    pt_flat = page_table.reshape(-1)

    def page_spec(i, shape):
        return pl.BlockSpec((None,) + shape, lambda b, j, pt: (pt[b * n_pages + j * pps + i], 0, 0))

    per = lambda r, w: pl.BlockSpec((None, r, w), lambda b, j, pt: (b, 0, 0))
    in_specs = [per(n_new, FOX_INNER), per(rows8, FOX_INNER), per(rows8, FOX_INNER), per(FOX_HEADS, LANES)]
    in_specs += [page_spec(i, (page, FOX_INNER)) for i in range(pps)] * 2
    in_specs += [page_spec(i, (FOX_HEADS, LANES)) for i in range(pps)] * 2
    nrow = n_new * FOX_HEADS
    grid_spec = pltpu.PrefetchScalarGridSpec(
        num_scalar_prefetch=1,
        grid=(nb, n_pages // pps),
        in_specs=in_specs,
        out_specs=per(n_new, FOX_INNER),
        scratch_shapes=[pltpu.VMEM((nrow, FOX_INNER), BF16), pltpu.VMEM((nrow, 1), F32), pltpu.VMEM((nrow, 1), F32),
                        pltpu.VMEM((nrow, FOX_INNER), F32), pltpu.VMEM((FOX_HEADS, LANES), F32),
                        pltpu.VMEM((BLOCK, FOX_INNER), BF16), pltpu.VMEM((BLOCK, FOX_INNER), BF16)],
    )
    return pl.pallas_call(
        functools.partial(_fox_sample_body, pps=pps, n_new=n_new),
        grid_spec=grid_spec,
        out_shape=jax.ShapeDtypeStruct((nb, n_new, FOX_INNER), F32),
        compiler_params=_params(2, 48),
        name="fox_sample",
    )(pt_flat, q, k_new8, v_new8, lf_new_t, *([cache_k] * pps), *([cache_v] * pps), *([c_page] * pps),
      *([tot_page] * pps))


def _ffn_weights(w_gate, w_up, w_down):
    nchunk = D_FF // FF_CHUNK
    wg3 = w_gate.astype(BF16).reshape(D_MODEL, nchunk, FF_CHUNK).transpose(1, 0, 2)
    wu3 = w_up.astype(BF16).reshape(D_MODEL, nchunk, FF_CHUNK).transpose(1, 0, 2)
    wd3 = w_down.astype(BF16).reshape(nchunk, FF_CHUNK, D_MODEL)
    return wg3, wu3, wd3


def _pick_tile(n, candidates):
    for t in candidates:
        if n % t == 0:
            return t
    raise ValueError(f"no tile for {n}")


def kernel(x_prompt, x_sample, cache_k, cache_v, cache_logf, page_table, state_conv, state_ssm, meta_tokens, norm_ffn1, w_ffn1_gate, w_ffn1_up, w_ffn1_down, norm_mix, w_in, conv_w, conv_b, dt_bias, A_log, D_skip, ssd_norm, w_ssd_out, b_forget, w_fox_out, w_o, norm_ffn2, w_ffn2_gate, w_ffn2_up, w_ffn2_down, norm_final):
    batch, seq, _ = x_prompt.shape
    nb, n_new, _ = x_sample.shape
    depth, n_phys, page = cache_k.shape[0], cache_k.shape[1], cache_k.shape[2]
    assert depth == 1 and page == BLOCK and seq % BLOCK == 0 and n_new <= SUBLANES
    n_s = nb * n_new
    n_s_pad = -(-n_s // BLOCK) * BLOCK
    meta_blk = n_s_pad // BLOCK
    rows_p = batch * seq

    ffn1_w = _ffn_weights(w_ffn1_gate[0], w_ffn1_up[0], w_ffn1_down[0])
    ffn2_w = _ffn_weights(w_ffn2_gate[0], w_ffn2_up[0], w_ffn2_down[0])
    wz, wxbc, wdt, wq, wk, wv, wf, wgs, wgf = jnp.split(
        w_in[0], [1024, 2560, 2576, 3600, 4624, 5648, 5664, 6688], axis=1)
    w_main = jnp.concatenate([wz, wxbc, wq, wk, wv, wgs, wgf], axis=1).astype(BF16)
    w_small = jnp.pad(jnp.concatenate([wdt, wf], axis=1), ((0, 0), (0, LANES - SSD_HEADS - FOX_HEADS))).astype(BF16)
    bias_small = jnp.pad(jnp.concatenate([dt_bias[0], b_forget[0]]), (0, LANES - SSD_HEADS - FOX_HEADS))[None, :]
    row = lambda a: a.reshape(1, -1).astype(F32)
    ssd_w = (jnp.pad(conv_w[0], ((0, SUBLANES - CONV_WIDTH), (0, 0))), row(conv_b[0]),
             jnp.pad(-jnp.exp(A_log[0].astype(F32)), (0, LANES - SSD_HEADS))[None, :],
             row(jnp.repeat(D_skip[0], SSD_HEAD_DIM)), row(ssd_norm[0]), w_ssd_out[0].astype(BF16))
    w_fox_b, w_o_b = w_fox_out[0].astype(BF16), w_o[0].astype(BF16)

    xp = x_prompt.reshape(rows_p, D_MODEL)
    x_small = jnp.concatenate([x_sample.reshape(n_s, D_MODEL), jnp.zeros((n_s_pad - n_s + PAD, D_MODEL), F32),
                               meta_tokens.astype(F32)], axis=0)
    rows_s = n_s_pad + BLOCK

    tm_ffn = _pick_tile(rows_p, (512, 256, 128))
    tm_in = _pick_tile(rows_p, (256, 128))

    x1 = _ffn(xp, rows_p, tm_ffn, row(norm_ffn1[0]), *ffn1_w, name="ffn1")
    x1s = _ffn(x_small, rows_s, BLOCK, row(norm_ffn1[0]), *ffn1_w, name="ffn1_small")
    z, xbc, q, k, v, gs, gf, sm = _inproj(x1, tm_in, row(norm_mix[0]), w_main, w_small, bias_small, "inproj")
    zs, xbcs, qs, ks, vs, gss, gfs, sms = _inproj(x1s, BLOCK, row(norm_mix[0]), w_main, w_small, bias_small,
                                                  "inproj_small")

    o_ssd, st_p = _ssd_prompt(xbc, z, sm, xbcs, sms, meta_blk, ssd_w, batch, seq)

    tq = _pick_tile(seq, (256, 128))
    nq = seq // tq
    npair = FOX_INNER // LANES
    c_tok, c_meta = _logf_cumsum(sm, sms, meta_blk, batch, seq, _pick_tile(seq, (512, 256, 128)))
    lo, hi = SSD_HEADS, SSD_HEADS + FOX_HEADS
    c_tok_t = c_tok.reshape(batch, nq, tq, LANES)[..., lo:hi].reshape(batch, nq, tq, npair, 2).transpose(0, 3, 1, 4, 2)
    c_meta_t = c_meta[..., lo:hi].reshape(batch, BLOCK, npair, 2).transpose(0, 2, 3, 1)
    o_att = _fox_prompt(q, k, v, ks, vs, c_tok_t, c_meta_t, meta_blk, batch, seq, tq)

    y_p = _ffn(x1, rows_p, tm_in, row(norm_ffn2[0]), *ffn2_w, merge_args=(o_ssd, o_att, gs, gf, w_fox_b, w_o_b),
               final_norm_w=row(norm_final), name="merge_ffn2")

    pad8 = lambda a: jnp.pad(a[:n_s].reshape(nb, n_new, -1), ((0, 0), (0, SUBLANES - n_new), (0, 0)))
    hist8 = jnp.pad(state_conv[0].astype(F32), ((0, 0), (SUBLANES - (CONV_WIDTH - 1), 0), (0, 0)))
    rper = SSD_HEADS // SSD_GROUPS
    state0 = state_ssm[0].astype(F32).reshape(nb, SSD_GROUPS, rper, SSD_HEAD_DIM, SSD_STATE)
    state0 = state0.transpose(0, 1, 4, 2, 3).reshape(nb, SSD_GROUPS, SSD_STATE, GROUP_W)
    o_ssd_s8, st_s = _ssd_sample(pad8(xbcs), pad8(zs), pad8(sms), hist8, state0, ssd_w, n_new)

    lf_t = cache_logf[0].astype(F32).transpose(0, 2, 1).reshape(n_phys * FOX_HEADS, page)
    c_page, tot_page = _page_cumsum(lf_t, _pick_tile(n_phys * FOX_HEADS, (4096, 2048, 1024, 512, 256, 128, 16)))
    c_page = c_page.reshape(n_phys, FOX_HEADS, page)
    tot_page = tot_page.reshape(n_phys, FOX_HEADS, page)
    lf_new_t = jnp.pad(sms[:n_s, lo:hi].reshape(nb, n_new, FOX_HEADS).transpose(0, 2, 1),
                       ((0, 0), (0, 0), (0, LANES - n_new)))
    pps = _pick_tile(page_table.shape[1], (8, 4, 2, 1))
    o_att_s = _fox_sample(page_table, qs[:n_s].reshape(nb, n_new, FOX_INNER), pad8(ks), pad8(vs), lf_new_t,
                          cache_k[0].reshape(n_phys, page, FOX_INNER), cache_v[0].reshape(n_phys, page, FOX_INNER),
                          c_page, tot_page, pps)

    rows_pad = lambda a: jnp.pad(a.reshape(n_s, -1), ((0, n_s_pad - n_s), (0, 0)))
    y_s = _ffn(x1s, n_s_pad, BLOCK, row(norm_ffn2[0]), *ffn2_w,
               merge_args=(rows_pad(o_ssd_s8[:, :n_new]), rows_pad(o_att_s), gss, gfs, w_fox_b, w_o_b),
               final_norm_w=row(norm_final), name="merge_ffn2_small")

    meta_rows = slice(n_s_pad + PAD, n_s_pad + BLOCK)

    def with_meta(tok, small, width):
        m = jnp.broadcast_to(small[meta_rows][None], (batch, N_META, width))
        return jnp.concatenate([m, tok.reshape(batch, seq, width)], axis=1)

    def unpack_state(st, n):
        st = st.reshape(n, SSD_GROUPS, SSD_STATE, rper, SSD_HEAD_DIM).transpose(0, 1, 3, 4, 2)
        return st.reshape(1, n, SSD_HEADS, SSD_HEAD_DIM, SSD_STATE)

    y_prompt = y_p.reshape(batch, seq, D_MODEL)
    y_sample = y_s[:n_s].reshape(nb, n_new, D_MODEL)
    k_p = with_meta(k, ks, FOX_INNER).reshape(1, batch, seq + N_META, FOX_HEADS, FOX_HEAD_DIM)
    v_p = with_meta(v, vs, FOX_INNER).reshape(1, batch, seq + N_META, FOX_HEADS, FOX_HEAD_DIM)
    lf_p = with_meta(sm[:, lo:hi], sms[:, lo:hi], FOX_HEADS)[None]
    conv_p = xbc.reshape(batch, seq, CONV_DIM)[:, seq - (CONV_WIDTH - 1):][None]
    ssm_p = unpack_state(st_p, batch)
    k_s = ks[:n_s].reshape(1, nb, n_new, FOX_HEADS, FOX_HEAD_DIM)
    v_s = vs[:n_s].reshape(1, nb, n_new, FOX_HEADS, FOX_HEAD_DIM)
    lf_s = sms[:n_s, lo:hi].reshape(1, nb, n_new, FOX_HEADS)
    conv_s = jnp.concatenate([state_conv[0].astype(F32), xbcs[:n_s].reshape(nb, n_new, CONV_DIM)], axis=1)[:, n_new:][None]
    ssm_s = unpack_state(st_s, nb)
    return (y_prompt, y_sample, k_p, v_p, lf_p, conv_p, ssm_p, k_s, v_s, lf_s, conv_s, ssm_s)
```

```python
import functools

import jax
import jax.numpy as jnp
from jax import lax
from jax.experimental import pallas as pl
from jax.experimental.pallas import tpu as pltpu

F32 = jnp.float32
BF16 = jnp.bfloat16

D_MODEL = 1024
D_FF = 2816
NORM_EPS = 1e-6
N_META = 16
BLOCK = 128
PAD = BLOCK - N_META
SSD_HEADS = 16
SSD_HEAD_DIM = 64
SSD_INNER = SSD_HEADS * SSD_HEAD_DIM
SSD_STATE = 128
SSD_GROUPS = 2
GROUP_W = SSD_INNER // SSD_GROUPS
CONV_WIDTH = 4
CONV_DIM = SSD_INNER + 2 * SSD_GROUPS * SSD_STATE
FOX_HEADS = 16
FOX_HEAD_DIM = 64
FOX_INNER = FOX_HEADS * FOX_HEAD_DIM
FOX_SCALE = FOX_HEAD_DIM ** -0.5
NEG_INF = -1e30

LANES = 128
SUBLANES = 8
V7X_VMEM_BYTES = 64 * 1024 * 1024

FF_CHUNK = D_FF // 2
LOG2E = 1.4426950408889634
Q_SCALE = FOX_SCALE * LOG2E
HIST = SUBLANES


def _dot(a, b):
    return jnp.dot(a, b, preferred_element_type=F32)


def _dot_nt(a, b):
    return lax.dot_general(a, b, (((1,), (1,)), ((), ())), preferred_element_type=F32)


def _split3(x):
    hi = x.astype(BF16)
    r = x - hi.astype(F32)
    mid = r.astype(BF16)
    lo = (r - mid.astype(F32)).astype(BF16)
    return hi, mid, lo


def _dot_exact(a01, x):
    hi, mid, lo = _split3(x)
    return _dot(a01, hi) + _dot(a01, mid) + _dot(a01, lo)


def _dot_exact_r(x, b01):
    hi, mid, lo = _split3(x)
    return _dot(hi, b01) + _dot(mid, b01) + _dot(lo, b01)


def _transpose_exact(x):
    n = x.shape[0]
    eye = (lax.broadcasted_iota(jnp.int32, (n, n), 0) == lax.broadcasted_iota(jnp.int32, (n, n), 1)).astype(BF16)
    hi, mid, lo = _split3(x)
    return _dot_nt(eye, hi) + _dot_nt(eye, mid) + _dot_nt(eye, lo)


def _tril(n):
    return (lax.broadcasted_iota(jnp.int32, (n, n), 1) <= lax.broadcasted_iota(jnp.int32, (n, n), 0)).astype(BF16)


def _triu(n):
    return (lax.broadcasted_iota(jnp.int32, (n, n), 0) <= lax.broadcasted_iota(jnp.int32, (n, n), 1)).astype(BF16)


def _rms(x):
    return x * lax.rsqrt(jnp.mean(x * x, axis=-1, keepdims=True) + NORM_EPS)


def _silu(x):
    return x * jax.nn.sigmoid(x)


def _params(n_axes, vmem_mib):
    return pltpu.CompilerParams(dimension_semantics=("arbitrary",) * n_axes,
                                vmem_limit_bytes=min(vmem_mib * 1024 * 1024, V7X_VMEM_BYTES - 4 * 1024 * 1024))


def _resident(shape):
    nd = len(shape)
    return pl.BlockSpec(shape, lambda *_: (0,) * nd, pipeline_mode=pl.Buffered(1))


def _ffn_body(*refs, merge, final_norm):
    refs = list(refs)
    x_ref = refs.pop(0)
    if merge:
        os_ref, oa_ref, gs_ref, gf_ref, wf_ref, wo_ref = refs[:6]
        refs = refs[6:]
    nw_ref, wg_ref, wu_ref, wd_ref = refs[:4]
    refs = refs[4:]
    if final_norm:
        fnw_ref = refs.pop(0)
    o_ref, hn_ref = refs

    x = x_ref[...]
    if merge:
        o_fox = _dot(oa_ref[...], wf_ref[...])
        mix = jax.nn.sigmoid(gs_ref[...]) * os_ref[...] + jax.nn.sigmoid(gf_ref[...]) * o_fox
        x = x + _dot(mix.astype(BF16), wo_ref[...])
    hn_ref[...] = (_rms(x) * nw_ref[...]).astype(BF16)
    y = x
    for c in range(D_FF // FF_CHUNK):
        cols = slice(c * FF_CHUNK, (c + 1) * FF_CHUNK)
        g = _dot(hn_ref[...], wg_ref[:, cols])
        u = _dot(hn_ref[...], wu_ref[:, cols])
        y = y + 0.5 * _dot((_silu(g) * u).astype(BF16), wd_ref[cols, :])
    if final_norm:
        y = _rms(y) * fnw_ref[...]
    o_ref[...] = y


def _ffn(x, n_rows, tm, norm_w, wg, wu, wd, merge_args=None, final_norm_w=None, name="ffn"):
    merge = merge_args is not None
    row_spec = pl.BlockSpec((tm, D_MODEL), lambda i: (i, 0))
    args, specs = [x], [row_spec]
    if merge:
        o_ssd, o_att, g_ssd, g_fox, w_fox, w_o = merge_args
        args += [o_ssd, o_att, g_ssd, g_fox, w_fox, w_o]
        specs += [row_spec] * 4 + [_resident(w_fox.shape), _resident(w_o.shape)]
    args += [norm_w, wg, wu, wd]
    specs += [_resident(norm_w.shape), _resident(wg.shape), _resident(wu.shape), _resident(wd.shape)]
    if final_norm_w is not None:
        args.append(final_norm_w)
        specs.append(_resident(final_norm_w.shape))
    return pl.pallas_call(
        functools.partial(_ffn_body, merge=merge, final_norm=final_norm_w is not None),
        grid=(n_rows // tm,),
        in_specs=specs,
        out_specs=row_spec,
        out_shape=jax.ShapeDtypeStruct((n_rows, D_MODEL), F32),
        scratch_shapes=[pltpu.VMEM((tm, D_MODEL), BF16)],
        compiler_params=_params(1, 60),
        name=name,
    )(*args)


_MAIN_WIDTHS = (SSD_INNER, CONV_DIM, FOX_INNER, FOX_INNER, FOX_INNER, D_MODEL, D_MODEL)
_Q_INDEX = 2


def _inproj_body(x_ref, nw_ref, wm_ref, ws_ref, bias_ref, *refs):
    outs, sm_ref, hn_ref = refs[:7], refs[7], refs[8]
    hn_ref[...] = (_rms(x_ref[...]) * nw_ref[...]).astype(BF16)
    off = 0
    for i, (ref, w) in enumerate(zip(outs, _MAIN_WIDTHS)):
        val = _dot(hn_ref[...], wm_ref[:, off:off + w])
        ref[...] = (val * Q_SCALE).astype(BF16) if i == _Q_INDEX else val
        off += w
    raw = _dot(hn_ref[...], ws_ref[...]) + bias_ref[...]
    t = jnp.log1p(jnp.exp(-jnp.abs(raw)))
    lane = lax.broadcasted_iota(jnp.int32, raw.shape, 1)
    sm_ref[...] = jnp.where(lane < SSD_HEADS, jnp.maximum(raw, 0.0) + t,
                            jnp.where(lane < SSD_HEADS + FOX_HEADS, jnp.minimum(raw, 0.0) - t, 0.0))


def _inproj(x, tm, norm_w, w_main, w_small, bias_small, name):
    rows = x.shape[0]
    row = lambda w: pl.BlockSpec((tm, w), lambda i: (i, 0))
    widths = _MAIN_WIDTHS + (LANES,)
    return pl.pallas_call(
        _inproj_body,
        grid=(rows // tm,),
        in_specs=[row(D_MODEL), _resident(norm_w.shape), _resident(w_main.shape), _resident(w_small.shape),
                  _resident(bias_small.shape)],
        out_specs=[row(w) for w in widths],
        out_shape=[jax.ShapeDtypeStruct((rows, w), BF16 if i == _Q_INDEX else F32) for i, w in enumerate(widths)],
        scratch_shapes=[pltpu.VMEM((tm, D_MODEL), BF16)],
        compiler_params=_params(1, 60),
        name=name,
    )(x, norm_w, w_main, w_small, bias_small)


def _cumsum_rows(x):
    return _dot_exact(_tril(x.shape[0]), x)


def _logf_cumsum_body(lt_ref, lm_ref, ct_ref, cm_ref, carry_ref):
    @pl.when(pl.program_id(1) == 0)
    def _():
        row = lax.broadcasted_iota(jnp.int32, (BLOCK, LANES), 0)
        cm = _cumsum_rows(jnp.where(row >= PAD, lm_ref[...], 0.0))
        cm_ref[...] = cm
        carry_ref[...] = cm[BLOCK - 1:BLOCK, :]

    c = _cumsum_rows(lt_ref[...]) + carry_ref[...]
    ct_ref[...] = c
    carry_ref[...] = c[c.shape[0] - 1:, :]


def _logf_cumsum(sm_tok, sm_small, meta_blk, batch, seq, tb):
    nt = seq // tb
    return pl.pallas_call(
        _logf_cumsum_body,
        grid=(batch, nt),
        in_specs=[pl.BlockSpec((tb, LANES), lambda b, j: (b * nt + j, 0)),
                  pl.BlockSpec((BLOCK, LANES), lambda b, j: (meta_blk, 0))],
        out_specs=[pl.BlockSpec((tb, LANES), lambda b, j: (b * nt + j, 0)),
                   pl.BlockSpec((None, BLOCK, LANES), lambda b, j: (b, 0, 0))],
        out_shape=[jax.ShapeDtypeStruct((batch * seq, LANES), F32),
                   jax.ShapeDtypeStruct((batch, BLOCK, LANES), F32)],
        scratch_shapes=[pltpu.VMEM((1, LANES), F32)],
        compiler_params=_params(2, 32),
        name="logf_cumsum",
    )(sm_tok, sm_small)


_BIAS_PIECES = 3


def _fox_prompt_body(q_ref, kt_ref, vt_ref, ct_ref, km_ref, vm_ref, cm_ref, o_ref,
                     k0_sc, k1_sc, v0_sc, v1_sc, km0_sc, km1_sc, vm0_sc, vm1_sc, *, tq):
    hp = pl.program_id(1)
    qi = pl.program_id(2)
    lane = lax.broadcasted_iota(jnp.int32, (1, LANES), 1)
    is0 = lane < FOX_HEAD_DIM
    extra0, extra1 = FOX_HEAD_DIM, 0

    @pl.when(qi == 0)
    def _():
        src = lax.broadcasted_iota(jnp.int32, (_BIAS_PIECES * LANES, LANES), 0)
        dst = lax.broadcasted_iota(jnp.int32, (_BIAS_PIECES * LANES, LANES), 1)
        c_lane = SSD_HEADS + 2 * hp

        def route(head, base):
            return jnp.where((src % LANES == c_lane + head) & (dst == base + src // LANES), 1.0, 0.0).astype(BF16)

        route0, route1 = route(0, extra0), route(1, extra1)
        eye = (lax.broadcasted_iota(jnp.int32, (LANES, LANES), 0)
               == lax.broadcasted_iota(jnp.int32, (LANES, LANES), 1)).astype(BF16)
        one0 = jnp.where(lane == extra0, 1.0, 0.0)
        one1 = jnp.where(lane == extra1, 1.0, 0.0)

        def prep(k, v, bias):
            pieces = jnp.concatenate(_split3(bias), axis=1)
            k0 = (jnp.where(is0, k, 0.0) + _dot(pieces, route0)).astype(BF16)
            k1 = (jnp.where(is0, 0.0, k) + _dot(pieces, route1)).astype(BF16)
            v0 = _dot_nt(eye, jnp.where(is0, v, one0).astype(BF16)).astype(BF16)
            v1 = _dot_nt(eye, jnp.where(is0, one1, v).astype(BF16)).astype(BF16)
            return k0, k1, v0, v1

        row = lax.broadcasted_iota(jnp.int32, (BLOCK, LANES), 0)
        bias_m = jnp.where(row >= PAD, cm_ref[...] * (-LOG2E), NEG_INF)
        km0_sc[...], km1_sc[...], vm0_sc[...], vm1_sc[...] = prep(km_ref[...], vm_ref[...], bias_m)

        def prep_block(i, carry):
            rows = pl.ds(pl.multiple_of(i * tq, tq), tq)
            k0_sc[i], k1_sc[i], v0_sc[i], v1_sc[i] = prep(kt_ref[rows, :], vt_ref[rows, :],
                                                         ct_ref[rows, :] * (-LOG2E))
            return carry

        lax.fori_loop(0, k0_sc.shape[0], prep_block, 0)

    q = q_ref[...].astype(F32)
    ones0 = jnp.where((lane >= extra0) & (lane < extra0 + _BIAS_PIECES), 1.0, 0.0)
    ones1 = jnp.where((lane >= extra1) & (lane < extra1 + _BIAS_PIECES), 1.0, 0.0)
    q0 = jnp.where(is0, q, ones0).astype(BF16)
    q1 = jnp.where(is0, ones1, q).astype(BF16)

    def head_update(m, acc, qh, kb, vt, mask):
        s = _dot_nt(kb, qh)
        if mask is not None:
            s = jnp.where(mask, s, NEG_INF)
        m_new = jnp.maximum(m, jnp.max(s, axis=0, keepdims=True))
        p = jnp.exp2(s - m_new).astype(BF16)
        return m_new, acc * jnp.exp2(m - m_new) + _dot(vt, p)

    def update(carry, k0b, k1b, v0t, v1t, mask):
        m0, acc0, m1, acc1 = carry
        m0, acc0 = head_update(m0, acc0, q0, k0b, v0t, mask)
        m1, acc1 = head_update(m1, acc1, q1, k1b, v1t, mask)
        return m0, acc0, m1, acc1

    neg = jnp.full((1, tq), NEG_INF, F32)
    zero = jnp.zeros((LANES, tq), F32)
    carry = update((neg, zero, neg, zero), km0_sc[...], km1_sc[...], vm0_sc[...], vm1_sc[...], None)
    carry = lax.fori_loop(0, qi, lambda j, c: update(c, k0_sc[j], k1_sc[j], v0_sc[j], v1_sc[j], None), carry)
    causal = lax.broadcasted_iota(jnp.int32, (tq, 1), 0) <= lax.broadcasted_iota(jnp.int32, (1, tq), 1)
    _, acc0, _, acc1 = update(carry, k0_sc[qi], k1_sc[qi], v0_sc[qi], v1_sc[qi], causal)
    top = lax.broadcasted_iota(jnp.int32, (LANES, 1), 0) < FOX_HEAD_DIM
    o_t = jnp.where(top, acc0 / acc0[extra0:extra0 + 1, :], acc1 / acc1[extra1:extra1 + 1, :])
    eye_q = (lax.broadcasted_iota(jnp.int32, (tq, tq), 0) == lax.broadcasted_iota(jnp.int32, (tq, tq), 1)).astype(BF16)
    o_ref[...] = _dot_nt(eye_q, o_t.astype(BF16)).astype(BF16)


def _fox_prompt(q, k, v, c_tok, k_small, v_small, c_meta, meta_blk, batch, seq, tq):
    nq = seq // tq
    npair = FOX_INNER // LANES
    kv_spec = pl.BlockSpec((seq, LANES), lambda b, hp, i: (b, hp))
    meta_spec = pl.BlockSpec((BLOCK, LANES), lambda b, hp, i: (meta_blk, hp))
    blk_spec = pl.BlockSpec((tq, LANES), lambda b, hp, i: (b * nq + i, hp))
    return pl.pallas_call(
        functools.partial(_fox_prompt_body, tq=tq),
        grid=(batch, npair, nq),
        in_specs=[blk_spec, kv_spec, kv_spec, pl.BlockSpec((seq, LANES), lambda b, hp, i: (b, 0)),
                  meta_spec, meta_spec, pl.BlockSpec((None, BLOCK, LANES), lambda b, hp, i: (b, 0, 0))],
        out_specs=blk_spec,
        out_shape=jax.ShapeDtypeStruct((batch * seq, FOX_INNER), BF16),
        scratch_shapes=[pltpu.VMEM((nq, tq, LANES), BF16)] * 2 + [pltpu.VMEM((nq, LANES, tq), BF16)] * 2
                       + [pltpu.VMEM((BLOCK, LANES), BF16)] * 4,
        compiler_params=_params(3, 48),
        name="fox_prompt",
    )(q, k, v, c_tok, k_small, v_small, c_meta)


def _ssd_chunk(xbuf_ref, dt, z, st_ref, ybuf_ref, cw_ref, cb_ref, a_ref, dskip_ref, nw_ref, wout_ref):
    L = BLOCK
    conv = cb_ref[...]
    for w in range(CONV_WIDTH):
        conv = conv + xbuf_ref[HIST - (CONV_WIDTH - 1) + w:HIST - (CONV_WIDTH - 1) + w + L, :] * cw_ref[w:w + 1, :]
    xc = _silu(conv)
    xs = xc[:, :SSD_INNER]
    gn = SSD_GROUPS * SSD_STATE

    a_col = _cumsum_rows(dt * a_ref[...])
    a_row = _transpose_exact(a_col)
    dt_row = _transpose_exact(dt)
    lane_head = lax.broadcasted_iota(jnp.int32, (LANES, SSD_HEADS * LANES), 1) // LANES
    e_wide = (lane_head == lax.broadcasted_iota(jnp.int32, (LANES, SSD_HEADS * LANES), 0)).astype(BF16)
    a_bc = _dot_exact_r(a_col, e_wide)
    col_head = lax.broadcasted_iota(jnp.int32, (LANES, SSD_INNER), 1) // SSD_HEAD_DIM
    e_head = (col_head == lax.broadcasted_iota(jnp.int32, (LANES, SSD_INNER), 0)).astype(BF16)
    a_exp = _dot_exact_r(a_col, e_head)
    decay_in = jnp.exp(a_exp)
    a_last = a_exp[L - 1:L, :]
    to_end = jnp.exp(a_last - a_exp) * _dot_exact_r(dt, e_head)
    xw = (xs * to_end).astype(BF16)
    chunk_decay = decay_in[L - 1:L, :]

    causal = lax.broadcasted_iota(jnp.int32, (L, L), 1) <= lax.broadcasted_iota(jnp.int32, (L, L), 0)
    is0 = lax.broadcasted_iota(jnp.int32, (1, LANES), 1) < SSD_HEAD_DIM
    heads_per_group = SSD_HEADS // SSD_GROUPS
    for g in range(SSD_GROUPS):
        b_g = xc[:, SSD_INNER + g * SSD_STATE:SSD_INNER + (g + 1) * SSD_STATE].astype(BF16)
        c_g = xc[:, SSD_INNER + gn + g * SSD_STATE:SSD_INNER + gn + (g + 1) * SSD_STATE].astype(BF16)
        cb = _dot_nt(c_g, b_g)
        lo, hi = g * GROUP_W, (g + 1) * GROUP_W
        h_in = st_ref[g]
        y_off = _dot(c_g, h_in.astype(BF16)) * decay_in[:, lo:hi]
        ybuf_ref[:, lo:hi] = y_off
        eye = (lax.broadcasted_iota(jnp.int32, (L, L), 0) == lax.broadcasted_iota(jnp.int32, (L, L), 1)).astype(BF16)
        b_t = _dot_nt(eye, b_g).astype(BF16)
        st_ref[g] = h_in * chunk_decay[:, lo:hi] + _dot(b_t, xw[:, lo:hi])
        for pair in range(heads_per_group // 2):
            h0 = g * heads_per_group + 2 * pair
            c0 = h0 * SSD_HEAD_DIM
            x_pair = xs[:, c0:c0 + LANES]
            y_pair = jnp.zeros((L, LANES), F32)
            for k, xm in ((0, jnp.where(is0, x_pair, 0.0)), (1, jnp.where(is0, 0.0, x_pair))):
                h = h0 + k
                seg = a_bc[:, h * LANES:(h + 1) * LANES] - a_row[h:h + 1, :]
                wgt = cb * jnp.exp(jnp.where(causal, seg, -jnp.inf)) * dt_row[h:h + 1, :]
                y_pair = y_pair + _dot(wgt.astype(BF16), xm.astype(BF16))
            ybuf_ref[:, c0:c0 + LANES] += y_pair

    y = (ybuf_ref[...] + xs * dskip_ref[...]) * _silu(z)
    yn = (_rms(y) * nw_ref[...]).astype(BF16)
    return _dot(yn, wout_ref[...])


_SSD_WEIGHT_SHAPES = ((SUBLANES, CONV_DIM), (1, CONV_DIM), (1, LANES), (1, SSD_INNER), (1, SSD_INNER),
                      (SSD_INNER, D_MODEL))


def _ssd_prompt_body(xt_ref, zt_ref, st_ref_in, xm_ref, sm_ref, cw_ref, cb_ref, a_ref, dskip_ref, nw_ref, wout_ref,
                     o_ref, state_ref, xbuf_ref, st_ref, ybuf_ref):
    c = pl.program_id(1)
    lane = lax.broadcasted_iota(jnp.int32, (BLOCK, LANES), 1)
    row = lax.broadcasted_iota(jnp.int32, (BLOCK, LANES), 0)

    @pl.when(c == 0)
    def _():
        st_ref[...] = jnp.zeros_like(st_ref)
        xbuf_ref[0:HIST, :] = jnp.zeros((HIST, CONV_DIM), F32)
        xbuf_ref[HIST:, :] = xm_ref[...]

    @pl.when(c > 0)
    def _():
        xbuf_ref[HIST:, :] = xt_ref[...]

    dt = jnp.where(c == 0,
                   jnp.where((lane < SSD_HEADS) & (row >= PAD), sm_ref[...], 0.0),
                   jnp.where(lane < SSD_HEADS, st_ref_in[...], 0.0))
    o_ref[...] = _ssd_chunk(xbuf_ref, dt, zt_ref[...], st_ref, ybuf_ref, cw_ref, cb_ref, a_ref, dskip_ref, nw_ref,
                            wout_ref)
    xbuf_ref[0:HIST, :] = xbuf_ref[BLOCK:BLOCK + HIST, :]

    @pl.when(c == pl.num_programs(1) - 1)
    def _():
        state_ref[...] = st_ref[...]


def _ssd_prompt(xbc, z, sm, xbc_small, sm_small, meta_blk, weights, batch, seq):
    nc = seq // BLOCK
    tok = lambda w: pl.BlockSpec((BLOCK, w), lambda b, c: (b * nc + jnp.maximum(c - 1, 0), 0))
    meta = lambda w: pl.BlockSpec((BLOCK, w), lambda b, c: (meta_blk, 0))
    return pl.pallas_call(
        _ssd_prompt_body,
        grid=(batch, nc + 1),
        in_specs=[tok(CONV_DIM), tok(SSD_INNER), tok(LANES), meta(CONV_DIM), meta(LANES)]
                 + [_resident(s) for s in _SSD_WEIGHT_SHAPES],
        out_specs=[tok(D_MODEL),
                   pl.BlockSpec((None, SSD_GROUPS, SSD_STATE, GROUP_W), lambda b, c: (b, 0, 0, 0))],
        out_shape=[jax.ShapeDtypeStruct((batch * seq, D_MODEL), F32),
                   jax.ShapeDtypeStruct((batch, SSD_GROUPS, SSD_STATE, GROUP_W), F32)],
        scratch_shapes=[pltpu.VMEM((HIST + BLOCK, CONV_DIM), F32),
                        pltpu.VMEM((SSD_GROUPS, SSD_STATE, GROUP_W), F32),
                        pltpu.VMEM((BLOCK, SSD_INNER), F32)],
        compiler_params=_params(2, 40),
        name="ssd_prompt",
    )(xbc, z, sm, xbc_small, sm_small, *weights)


def _ssd_sample_body(x_ref, z_ref, sm_ref, hist_ref, state0_ref, cw_ref, cb_ref, a_ref, dskip_ref, nw_ref, wout_ref,
                     o_ref, state_ref, xbuf_ref, st_ref, ybuf_ref, dt_ref, zbuf_ref, *, n_new):
    rows = x_ref.shape[0]
    xbuf_ref[0:HIST, :] = hist_ref[...]
    xbuf_ref[HIST:HIST + rows, :] = x_ref[...]
    xbuf_ref[HIST + rows:, :] = jnp.zeros((BLOCK - rows, CONV_DIM), F32)
    lane = lax.broadcasted_iota(jnp.int32, (rows, LANES), 1)
    row = lax.broadcasted_iota(jnp.int32, (rows, LANES), 0)
    dt_ref[...] = jnp.zeros_like(dt_ref)
    dt_ref[0:rows, :] = jnp.where((lane < SSD_HEADS) & (row < n_new), sm_ref[...], 0.0)
    zbuf_ref[...] = jnp.zeros_like(zbuf_ref)
    zbuf_ref[0:rows, :] = z_ref[...]
    st_ref[...] = state0_ref[...]
    out = _ssd_chunk(xbuf_ref, dt_ref[...], zbuf_ref[...], st_ref, ybuf_ref, cw_ref, cb_ref, a_ref, dskip_ref, nw_ref,
                     wout_ref)
    o_ref[...] = out[0:rows, :]
    state_ref[...] = st_ref[...]


def _ssd_sample(x8, z8, sm8, hist8, state0, weights, n_new):
    nb, rows = x8.shape[0], x8.shape[1]
    per = lambda w: pl.BlockSpec((None, rows, w), lambda b: (b, 0, 0))
    st_spec = pl.BlockSpec((None, SSD_GROUPS, SSD_STATE, GROUP_W), lambda b: (b, 0, 0, 0))
    return pl.pallas_call(
        functools.partial(_ssd_sample_body, n_new=n_new),
        grid=(nb,),
        in_specs=[per(CONV_DIM), per(SSD_INNER), per(LANES), per(CONV_DIM), st_spec]
                 + [_resident(s) for s in _SSD_WEIGHT_SHAPES],
        out_specs=[per(D_MODEL), st_spec],
        out_shape=[jax.ShapeDtypeStruct((nb, rows, D_MODEL), F32),
                   jax.ShapeDtypeStruct((nb, SSD_GROUPS, SSD_STATE, GROUP_W), F32)],
        scratch_shapes=[pltpu.VMEM((HIST + BLOCK, CONV_DIM), F32),
                        pltpu.VMEM((SSD_GROUPS, SSD_STATE, GROUP_W), F32),
                        pltpu.VMEM((BLOCK, SSD_INNER), F32),
                        pltpu.VMEM((BLOCK, LANES), F32),
                        pltpu.VMEM((BLOCK, SSD_INNER), F32)],
        compiler_params=_params(1, 40),
        name="ssd_sample",
    )(x8, z8, sm8, hist8, state0, *weights)


def _page_cumsum_body(x_ref, c_ref, tot_ref):
    x = x_ref[...]
    n = x.shape[1]
    lane = lax.broadcasted_iota(jnp.int32, (1, n), 1)
    c, t = x, x
    shift = FOX_HEADS
    while shift < n:
        c = c + jnp.where(lane >= shift, pltpu.roll(c, shift, axis=1), 0.0)
        t = t + pltpu.roll(t, shift, axis=1)
        shift *= 2
    c_ref[...] = c
    tot_ref[...] = t


def _page_cumsum(lf_flat, tr):
    rows, n = lf_flat.shape
    spec = pl.BlockSpec((tr, n), lambda i: (i, 0))
    return pl.pallas_call(
        _page_cumsum_body,
        grid=(rows // tr,),
        in_specs=[spec],
        out_specs=[spec, spec],
        out_shape=[jax.ShapeDtypeStruct((rows, n), F32)] * 2,
        compiler_params=_params(1, 40),
        name="page_cumsum",
    )(lf_flat)


def _fox_sample_body(pt_ref, q_ref, kn_ref, vn_ref, ln_ref, *refs, pps, n_new):
    k_refs, v_refs, c_refs, t_refs = refs[:pps], refs[pps:2 * pps], refs[2 * pps:3 * pps], refs[3 * pps:4 * pps]
    o_ref, m_sc, l_sc, acc_sc, carry_sc, mask_sc = refs[4 * pps:]
    del pt_ref
    j = pl.program_id(1)
    nrow = n_new * FOX_HEADS
    ncol = mask_sc.shape[1]

    @pl.when(j == 0)
    def _():
        m_sc[...] = jnp.full_like(m_sc, NEG_INF)
        l_sc[...] = jnp.zeros_like(l_sc)
        acc_sc[...] = jnp.zeros_like(acc_sc)
        carry_sc[...] = jnp.zeros_like(carry_sc)
        same_head = (lax.broadcasted_iota(jnp.int32, (nrow, ncol), 0) % FOX_HEADS
                     == lax.broadcasted_iota(jnp.int32, (nrow, ncol), 1) % FOX_HEADS)
        mask_sc[...] = jnp.where(same_head, 0.0, NEG_INF)

    q = q_ref[...]

    def accumulate(s_list, v_list):
        m_old = m_sc[...]
        m_new = m_old
        for s in s_list:
            m_new = jnp.maximum(m_new, jnp.max(s, axis=1, keepdims=True))
        alpha = jnp.exp2(m_old - m_new)
        l_new = alpha * l_sc[...]
        pv = jnp.zeros(acc_sc.shape, F32)
        for s, v in zip(s_list, v_list):
            p = jnp.exp2(s - m_new)
            l_new = l_new + jnp.sum(p, axis=1, keepdims=True)
            pv = pv + _dot(p.astype(BF16), v)
        m_sc[...] = m_new
        l_sc[...] = l_new
        acc_sc[...] = acc_sc[...] * alpha + pv

    carry = carry_sc[...]
    s_list, v_list = [], []
    for i in range(pps):
        k2 = k_refs[i][...].reshape(ncol, FOX_HEAD_DIM).astype(BF16)
        bias = (carry + c_refs[i][...]) * LOG2E
        s_list.append(_dot_nt(q, k2) + (mask_sc[...] - bias))
        v_list.append(v_refs[i][...].reshape(ncol, FOX_HEAD_DIM).astype(BF16))
        carry = carry + t_refs[i][...]
    carry_sc[...] = carry
    accumulate(s_list, v_list)

    @pl.when(j == pl.num_programs(1) - 1)
    def _():
        r = lax.broadcasted_iota(jnp.int32, (LANES, LANES), 0)
        c = lax.broadcasted_iota(jnp.int32, (LANES, LANES), 1)
        scan = jnp.where((r % FOX_HEADS == c % FOX_HEADS) & (r <= c), 1.0, 0.0).astype(BF16)
        cum = _dot_exact_r(ln_ref[...], scan)[0:1, :]
        bias = (carry_sc[:, 0:LANES] + cum) * LOG2E
        row = lax.broadcasted_iota(jnp.int32, (nrow, LANES), 0)
        col = lax.broadcasted_iota(jnp.int32, (nrow, LANES), 1)
        valid = (col < nrow) & (col % FOX_HEADS == row % FOX_HEADS) & (col // FOX_HEADS <= row // FOX_HEADS)
        s = jnp.where(valid, _dot_nt(q, kn_ref[...].astype(BF16)) - bias, NEG_INF)
        accumulate([s], [vn_ref[...].astype(BF16)])
        o_ref[...] = acc_sc[...] / l_sc[...]


def _fox_sample(page_table, q2, k_new2, v_new2, lf_new_flat, cache_k, cache_v, c_flat, tot_flat, pps):
    nb, nrow, _ = q2.shape
    n_new = nrow // FOX_HEADS
    n_pages = page_table.shape[1]
    page = cache_k.shape[2]
    ncol = page * FOX_HEADS
    pt_flat = page_table.reshape(-1)

    def page_idx(i):
        return lambda b, j, pt: pt[b * n_pages + j * pps + i]

    def kv_spec(i):
        f = page_idx(i)
        return pl.BlockSpec((None, None, page, FOX_HEADS, FOX_HEAD_DIM), lambda b, j, pt: (0, f(b, j, pt), 0, 0, 0))

    def c_spec(i):
        f = page_idx(i)
        return pl.BlockSpec((None, 1, ncol), lambda b, j, pt: (f(b, j, pt), 0, 0))

    per = lambda r, w: pl.BlockSpec((None, r, w), lambda b, j, pt: (b, 0, 0))
    in_specs = [per(nrow, FOX_HEAD_DIM), per(LANES, FOX_HEAD_DIM), per(LANES, FOX_HEAD_DIM), per(SUBLANES, LANES)]
    in_specs += [kv_spec(i) for i in range(pps)] * 2 + [c_spec(i) for i in range(pps)] * 2
    grid_spec = pltpu.PrefetchScalarGridSpec(
        num_scalar_prefetch=1,
        grid=(nb, n_pages // pps),
        in_specs=in_specs,
        out_specs=per(nrow, FOX_HEAD_DIM),
        scratch_shapes=[pltpu.VMEM((nrow, 1), F32), pltpu.VMEM((nrow, 1), F32), pltpu.VMEM((nrow, FOX_HEAD_DIM), F32),
                        pltpu.VMEM((1, ncol), F32), pltpu.VMEM((nrow, ncol), F32)],
    )
    return pl.pallas_call(
        functools.partial(_fox_sample_body, pps=pps, n_new=n_new),
        grid_spec=grid_spec,
        out_shape=jax.ShapeDtypeStruct((nb, nrow, FOX_HEAD_DIM), F32),
        compiler_params=_params(2, 60),
        name="fox_sample",
    )(pt_flat, q2, k_new2, v_new2, lf_new_flat, *([cache_k] * pps), *([cache_v] * pps), *([c_flat] * pps),
      *([tot_flat] * pps))


def _ffn_weights(w_gate, w_up, w_down):
    return w_gate.astype(BF16), w_up.astype(BF16), w_down.astype(BF16)


def _pick_tile(n, candidates):
    for t in candidates:
        if n % t == 0:
            return t
    raise ValueError(f"no tile for {n}")


def kernel(x_prompt, x_sample, cache_k, cache_v, cache_logf, page_table, state_conv, state_ssm, meta_tokens, norm_ffn1, w_ffn1_gate, w_ffn1_up, w_ffn1_down, norm_mix, w_in, conv_w, conv_b, dt_bias, A_log, D_skip, ssd_norm, w_ssd_out, b_forget, w_fox_out, w_o, norm_ffn2, w_ffn2_gate, w_ffn2_up, w_ffn2_down, norm_final):
    batch, seq, _ = x_prompt.shape
    nb, n_new, _ = x_sample.shape
    depth, n_phys, page = cache_k.shape[0], cache_k.shape[1], cache_k.shape[2]
    assert depth == 1 and page == BLOCK and seq % BLOCK == 0 and n_new <= SUBLANES
    n_s = nb * n_new
    n_s_pad = -(-n_s // BLOCK) * BLOCK
    meta_blk = n_s_pad // BLOCK
    rows_p = batch * seq

    ffn1_w = _ffn_weights(w_ffn1_gate[0], w_ffn1_up[0], w_ffn1_down[0])
    ffn2_w = _ffn_weights(w_ffn2_gate[0], w_ffn2_up[0], w_ffn2_down[0])
    wz, wxbc, wdt, wq, wk, wv, wf, wgs, wgf = jnp.split(
        w_in[0], [1024, 2560, 2576, 3600, 4624, 5648, 5664, 6688], axis=1)
    w_main = jnp.concatenate([wz, wxbc, wq, wk, wv, wgs, wgf], axis=1).astype(BF16)
    w_small = jnp.pad(jnp.concatenate([wdt, wf], axis=1), ((0, 0), (0, LANES - SSD_HEADS - FOX_HEADS))).astype(BF16)
    bias_small = jnp.pad(jnp.concatenate([dt_bias[0], b_forget[0]]), (0, LANES - SSD_HEADS - FOX_HEADS))[None, :]
    row = lambda a: a.reshape(1, -1).astype(F32)
    ssd_w = (jnp.pad(conv_w[0], ((0, SUBLANES - CONV_WIDTH), (0, 0))), row(conv_b[0]),
             jnp.pad(-jnp.exp(A_log[0].astype(F32)), (0, LANES - SSD_HEADS))[None, :],
             row(jnp.repeat(D_skip[0], SSD_HEAD_DIM)), row(ssd_norm[0]), w_ssd_out[0].astype(BF16))
    w_fox_b, w_o_b = w_fox_out[0].astype(BF16), w_o[0].astype(BF16)

    xp = x_prompt.reshape(rows_p, D_MODEL)
    x_small = jnp.concatenate([x_sample.reshape(n_s, D_MODEL), jnp.zeros((n_s_pad - n_s + PAD, D_MODEL), F32),
                               meta_tokens.astype(F32)], axis=0)
    rows_s = n_s_pad + BLOCK

    tm_ffn = _pick_tile(rows_p, (512, 256, 128))
    tm_in = _pick_tile(rows_p, (256, 128))

    x1 = _ffn(xp, rows_p, tm_ffn, row(norm_ffn1[0]), *ffn1_w, name="ffn1")
    x1s = _ffn(x_small, rows_s, BLOCK, row(norm_ffn1[0]), *ffn1_w, name="ffn1_small")
    z, xbc, q, k, v, gs, gf, sm = _inproj(x1, tm_in, row(norm_mix[0]), w_main, w_small, bias_small, "inproj")
    zs, xbcs, qs, ks, vs, gss, gfs, sms = _inproj(x1s, BLOCK, row(norm_mix[0]), w_main, w_small, bias_small,
                                                  "inproj_small")

    o_ssd, st_p = _ssd_prompt(xbc, z, sm, xbcs, sms, meta_blk, ssd_w, batch, seq)

    tq = _pick_tile(seq, (512, 256, 128))
    c_tok, c_meta = _logf_cumsum(sm, sms, meta_blk, batch, seq, _pick_tile(seq, (512, 256, 128)))
    lo, hi = SSD_HEADS, SSD_HEADS + FOX_HEADS
    o_att = _fox_prompt(q, k, v, c_tok, ks, vs, c_meta, meta_blk, batch, seq, tq)

    y_p = _ffn(x1, rows_p, tm_ffn, row(norm_ffn2[0]), *ffn2_w, merge_args=(o_ssd, o_att, gs, gf, w_fox_b, w_o_b),
               final_norm_w=row(norm_final), name="merge_ffn2")

    pad8 = lambda a: jnp.pad(a[:n_s].reshape(nb, n_new, -1), ((0, 0), (0, SUBLANES - n_new), (0, 0)))
    hist8 = jnp.pad(state_conv[0].astype(F32), ((0, 0), (SUBLANES - (CONV_WIDTH - 1), 0), (0, 0)))
    rper = SSD_HEADS // SSD_GROUPS
    state0 = state_ssm[0].astype(F32).reshape(nb, SSD_GROUPS, rper, SSD_HEAD_DIM, SSD_STATE)
    state0 = state0.transpose(0, 1, 4, 2, 3).reshape(nb, SSD_GROUPS, SSD_STATE, GROUP_W)
    o_ssd_s8, st_s = _ssd_sample(pad8(xbcs), pad8(zs), pad8(sms), hist8, state0, ssd_w, n_new)

    ncol = page * FOX_HEADS
    nrow = n_new * FOX_HEADS
    lf_flat = cache_logf[0].astype(F32).reshape(n_phys, ncol)
    c_flat, tot_flat = _page_cumsum(lf_flat, _pick_tile(n_phys, (512, 256, 128, 64, 32, 16, 8)))
    per_head = lambda a: a[:n_s].reshape(nb, nrow, FOX_HEAD_DIM)
    to_block = lambda a: jnp.pad(per_head(a), ((0, 0), (0, LANES - nrow), (0, 0)))
    lf_new_flat = jnp.pad(sms[:n_s, lo:hi].reshape(nb, 1, nrow), ((0, 0), (0, SUBLANES - 1), (0, LANES - nrow)))
    pps = _pick_tile(page_table.shape[1], (8, 4, 2, 1))
    o_att_s = _fox_sample(page_table, per_head(qs), to_block(ks), to_block(vs), lf_new_flat, cache_k, cache_v,
                          c_flat.reshape(n_phys, 1, ncol), tot_flat.reshape(n_phys, 1, ncol), pps)

    rows_pad = lambda a: jnp.pad(a.reshape(n_s, -1), ((0, n_s_pad - n_s), (0, 0)))
    y_s = _ffn(x1s, n_s_pad, BLOCK, row(norm_ffn2[0]), *ffn2_w,
               merge_args=(rows_pad(o_ssd_s8[:, :n_new]), rows_pad(o_att_s).astype(BF16), gss, gfs, w_fox_b, w_o_b),
               final_norm_w=row(norm_final), name="merge_ffn2_small")

    meta_rows = slice(n_s_pad + PAD, n_s_pad + BLOCK)

    def with_meta(tok, small, width):
        m = jnp.broadcast_to(small[meta_rows][None], (batch, N_META, width))
        return jnp.concatenate([m, tok.reshape(batch, seq, width)], axis=1)

    def unpack_state(st, n):
        st = st.reshape(n, SSD_GROUPS, SSD_STATE, rper, SSD_HEAD_DIM).transpose(0, 1, 3, 4, 2)
        return st.reshape(1, n, SSD_HEADS, SSD_HEAD_DIM, SSD_STATE)

    y_prompt = y_p.reshape(batch, seq, D_MODEL)
    y_sample = y_s[:n_s].reshape(nb, n_new, D_MODEL)
    k_p = with_meta(k, ks, FOX_INNER).reshape(1, batch, seq + N_META, FOX_HEADS, FOX_HEAD_DIM)
    v_p = with_meta(v, vs, FOX_INNER).reshape(1, batch, seq + N_META, FOX_HEADS, FOX_HEAD_DIM)
    lf_p = with_meta(sm[:, lo:hi], sms[:, lo:hi], FOX_HEADS)[None]
    conv_p = xbc.reshape(batch, seq, CONV_DIM)[:, seq - (CONV_WIDTH - 1):][None]
    ssm_p = unpack_state(st_p, batch)
    k_s = ks[:n_s].reshape(1, nb, n_new, FOX_HEADS, FOX_HEAD_DIM)
    v_s = vs[:n_s].reshape(1, nb, n_new, FOX_HEADS, FOX_HEAD_DIM)
    lf_s = sms[:n_s, lo:hi].reshape(1, nb, n_new, FOX_HEADS)
    conv_s = jnp.concatenate([state_conv[0].astype(F32), xbcs[:n_s].reshape(nb, n_new, CONV_DIM)], axis=1)[:, n_new:][None]
    ssm_s = unpack_state(st_s, nb)
    return (y_prompt, y_sample, k_p, v_p, lf_p, conv_p, ssm_p, k_s, v_s, lf_s, conv_s, ssm_s)
```

```python
import functools

import jax
import jax.numpy as jnp
from jax import lax
from jax.experimental import pallas as pl
from jax.experimental.pallas import tpu as pltpu

F32 = jnp.float32
BF16 = jnp.bfloat16

D_MODEL = 1024
D_FF = 2816
NORM_EPS = 1e-6
N_META = 16
BLOCK = 128
PAD = BLOCK - N_META
SSD_HEADS = 16
SSD_HEAD_DIM = 64
SSD_INNER = SSD_HEADS * SSD_HEAD_DIM
SSD_STATE = 128
SSD_GROUPS = 2
GROUP_W = SSD_INNER // SSD_GROUPS
CONV_WIDTH = 4
CONV_DIM = SSD_INNER + 2 * SSD_GROUPS * SSD_STATE
FOX_HEADS = 16
FOX_HEAD_DIM = 64
FOX_INNER = FOX_HEADS * FOX_HEAD_DIM
FOX_SCALE = FOX_HEAD_DIM ** -0.5
NEG_INF = -1e30

LANES = 128
SUBLANES = 8
V7X_VMEM_BYTES = 64 * 1024 * 1024

FF_CHUNK = D_FF // 2
LOG2E = 1.4426950408889634
Q_SCALE = FOX_SCALE * LOG2E
HIST = SUBLANES


def _dot(a, b):
    return jnp.dot(a, b, preferred_element_type=F32)


def _dot_nt(a, b):
    return lax.dot_general(a, b, (((1,), (1,)), ((), ())), preferred_element_type=F32)


def _split3(x):
    hi = x.astype(BF16)
    r = x - hi.astype(F32)
    mid = r.astype(BF16)
    lo = (r - mid.astype(F32)).astype(BF16)
    return hi, mid, lo


def _dot_exact(a01, x):
    hi, mid, lo = _split3(x)
    return _dot(a01, hi) + _dot(a01, mid) + _dot(a01, lo)


def _dot_exact_r(x, b01):
    hi, mid, lo = _split3(x)
    return _dot(hi, b01) + _dot(mid, b01) + _dot(lo, b01)


def _transpose_exact(x):
    n = x.shape[0]
    eye = (lax.broadcasted_iota(jnp.int32, (n, n), 0) == lax.broadcasted_iota(jnp.int32, (n, n), 1)).astype(BF16)
    hi, mid, lo = _split3(x)
    return _dot_nt(eye, hi) + _dot_nt(eye, mid) + _dot_nt(eye, lo)


def _tril(n):
    return (lax.broadcasted_iota(jnp.int32, (n, n), 1) <= lax.broadcasted_iota(jnp.int32, (n, n), 0)).astype(BF16)


def _triu(n):
    return (lax.broadcasted_iota(jnp.int32, (n, n), 0) <= lax.broadcasted_iota(jnp.int32, (n, n), 1)).astype(BF16)


def _rms(x):
    return x * lax.rsqrt(jnp.mean(x * x, axis=-1, keepdims=True) + NORM_EPS)


def _silu(x):
    return x * jax.nn.sigmoid(x)


def _params(n_axes, vmem_mib):
    return pltpu.CompilerParams(dimension_semantics=("arbitrary",) * n_axes,
                                vmem_limit_bytes=min(vmem_mib * 1024 * 1024, V7X_VMEM_BYTES - 4 * 1024 * 1024))


def _resident(shape):
    nd = len(shape)
    return pl.BlockSpec(shape, lambda *_: (0,) * nd, pipeline_mode=pl.Buffered(1))


def _ffn_body(*refs, merge, final_norm):
    refs = list(refs)
    x_ref = refs.pop(0)
    if merge:
        os_ref, oa_ref, gs_ref, gf_ref, wf_ref, wo_ref = refs[:6]
        refs = refs[6:]
    nw_ref, wg_ref, wu_ref, wd_ref = refs[:4]
    refs = refs[4:]
    if final_norm:
        fnw_ref = refs.pop(0)
    o_ref, hn_ref = refs

    x = x_ref[...]
    if merge:
        o_fox = _dot(oa_ref[...], wf_ref[...])
        mix = jax.nn.sigmoid(gs_ref[...]) * os_ref[...] + jax.nn.sigmoid(gf_ref[...]) * o_fox
        x = x + _dot(mix.astype(BF16), wo_ref[...])
    hn_ref[...] = (_rms(x) * nw_ref[...]).astype(BF16)
    y = x
    for c in range(D_FF // FF_CHUNK):
        cols = slice(c * FF_CHUNK, (c + 1) * FF_CHUNK)
        g = _dot(hn_ref[...], wg_ref[:, cols])
        u = _dot(hn_ref[...], wu_ref[:, cols])
        y = y + 0.5 * _dot((_silu(g) * u).astype(BF16), wd_ref[cols, :])
    if final_norm:
        y = _rms(y) * fnw_ref[...]
    o_ref[...] = y


def _ffn(x, n_rows, tm, norm_w, wg, wu, wd, merge_args=None, final_norm_w=None, name="ffn"):
    merge = merge_args is not None
    row_spec = pl.BlockSpec((tm, D_MODEL), lambda i: (i, 0))
    args, specs = [x], [row_spec]
    if merge:
        o_ssd, o_att, g_ssd, g_fox, w_fox, w_o = merge_args
        args += [o_ssd, o_att, g_ssd, g_fox, w_fox, w_o]
        specs += [row_spec] * 4 + [_resident(w_fox.shape), _resident(w_o.shape)]
    args += [norm_w, wg, wu, wd]
    specs += [_resident(norm_w.shape), _resident(wg.shape), _resident(wu.shape), _resident(wd.shape)]
    if final_norm_w is not None:
        args.append(final_norm_w)
        specs.append(_resident(final_norm_w.shape))
    return pl.pallas_call(
        functools.partial(_ffn_body, merge=merge, final_norm=final_norm_w is not None),
        grid=(n_rows // tm,),
        in_specs=specs,
        out_specs=row_spec,
        out_shape=jax.ShapeDtypeStruct((n_rows, D_MODEL), F32),
        scratch_shapes=[pltpu.VMEM((tm, D_MODEL), BF16)],
        compiler_params=_params(1, 60),
        name=name,
    )(*args)


_MAIN_WIDTHS = (SSD_INNER, CONV_DIM, FOX_INNER, FOX_INNER, FOX_INNER, D_MODEL, D_MODEL)
_Q_INDEX = 2


def _inproj_body(x_ref, nw_ref, wm_ref, ws_ref, bias_ref, *refs):
    outs, sm_ref, hn_ref = refs[:7], refs[7], refs[8]
    hn_ref[...] = (_rms(x_ref[...]) * nw_ref[...]).astype(BF16)
    off = 0
    for i, (ref, w) in enumerate(zip(outs, _MAIN_WIDTHS)):
        val = _dot(hn_ref[...], wm_ref[:, off:off + w])
        ref[...] = (val * Q_SCALE).astype(BF16) if i == _Q_INDEX else val
        off += w
    raw = _dot(hn_ref[...], ws_ref[...]) + bias_ref[...]
    t = jnp.log1p(jnp.exp(-jnp.abs(raw)))
    lane = lax.broadcasted_iota(jnp.int32, raw.shape, 1)
    sm_ref[...] = jnp.where(lane < SSD_HEADS, jnp.maximum(raw, 0.0) + t,
                            jnp.where(lane < SSD_HEADS + FOX_HEADS, jnp.minimum(raw, 0.0) - t, 0.0))


def _inproj(x, tm, norm_w, w_main, w_small, bias_small, name):
    rows = x.shape[0]
    row = lambda w: pl.BlockSpec((tm, w), lambda i: (i, 0))
    widths = _MAIN_WIDTHS + (LANES,)
    return pl.pallas_call(
        _inproj_body,
        grid=(rows // tm,),
        in_specs=[row(D_MODEL), _resident(norm_w.shape), _resident(w_main.shape), _resident(w_small.shape),
                  _resident(bias_small.shape)],
        out_specs=[row(w) for w in widths],
        out_shape=[jax.ShapeDtypeStruct((rows, w), BF16 if i == _Q_INDEX else F32) for i, w in enumerate(widths)],
        scratch_shapes=[pltpu.VMEM((tm, D_MODEL), BF16)],
        compiler_params=_params(1, 60),
        name=name,
    )(x, norm_w, w_main, w_small, bias_small)


def _cumsum_rows(x):
    return _dot_exact(_tril(x.shape[0]), x)


def _logf_cumsum_body(lt_ref, lm_ref, ct_ref, cm_ref, carry_ref):
    @pl.when(pl.program_id(1) == 0)
    def _():
        row = lax.broadcasted_iota(jnp.int32, (BLOCK, LANES), 0)
        cm = _cumsum_rows(jnp.where(row >= PAD, lm_ref[...], 0.0))
        cm_ref[...] = cm
        carry_ref[...] = cm[BLOCK - 1:BLOCK, :]

    c = _cumsum_rows(lt_ref[...]) + carry_ref[...]
    ct_ref[...] = c
    carry_ref[...] = c[c.shape[0] - 1:, :]


def _logf_cumsum(sm_tok, sm_small, meta_blk, batch, seq, tb):
    nt = seq // tb
    return pl.pallas_call(
        _logf_cumsum_body,
        grid=(batch, nt),
        in_specs=[pl.BlockSpec((tb, LANES), lambda b, j: (b * nt + j, 0)),
                  pl.BlockSpec((BLOCK, LANES), lambda b, j: (meta_blk, 0))],
        out_specs=[pl.BlockSpec((tb, LANES), lambda b, j: (b * nt + j, 0)),
                   pl.BlockSpec((None, BLOCK, LANES), lambda b, j: (b, 0, 0))],
        out_shape=[jax.ShapeDtypeStruct((batch * seq, LANES), F32),
                   jax.ShapeDtypeStruct((batch, BLOCK, LANES), F32)],
        scratch_shapes=[pltpu.VMEM((1, LANES), F32)],
        compiler_params=_params(2, 32),
        name="logf_cumsum",
    )(sm_tok, sm_small)


_BIAS_PIECES = 3
_Q_GROUP = 512


def _fox_prompt_body(q_ref, kt_ref, vt_ref, ct_ref, km_ref, vm_ref, cm_ref, o_ref,
                     k0_sc, k1_sc, v0_sc, v1_sc, km0_sc, km1_sc, vm0_sc, vm1_sc, *, tq):
    hp = pl.program_id(1)
    qi = pl.program_id(2)
    lane = lax.broadcasted_iota(jnp.int32, (1, LANES), 1)
    is0 = lane < FOX_HEAD_DIM
    extra0, extra1 = FOX_HEAD_DIM, 0

    @pl.when(qi == 0)
    def _():
        src = lax.broadcasted_iota(jnp.int32, (_BIAS_PIECES * LANES, LANES), 0)
        dst = lax.broadcasted_iota(jnp.int32, (_BIAS_PIECES * LANES, LANES), 1)
        c_lane = SSD_HEADS + 2 * hp

        def route(head, base):
            return jnp.where((src % LANES == c_lane + head) & (dst == base + src // LANES), 1.0, 0.0).astype(BF16)

        route0, route1 = route(0, extra0), route(1, extra1)
        eye = (lax.broadcasted_iota(jnp.int32, (LANES, LANES), 0)
               == lax.broadcasted_iota(jnp.int32, (LANES, LANES), 1)).astype(BF16)
        one0 = jnp.where(lane == extra0, 1.0, 0.0)
        one1 = jnp.where(lane == extra1, 1.0, 0.0)

        def prep(k, v, bias):
            pieces = jnp.concatenate(_split3(bias), axis=1)
            k0 = (jnp.where(is0, k, 0.0) + _dot(pieces, route0)).astype(BF16)
            k1 = (jnp.where(is0, 0.0, k) + _dot(pieces, route1)).astype(BF16)
            v0 = _dot_nt(eye, jnp.where(is0, v, one0).astype(BF16)).astype(BF16)
            v1 = _dot_nt(eye, jnp.where(is0, one1, v).astype(BF16)).astype(BF16)
            return k0, k1, v0, v1

        row = lax.broadcasted_iota(jnp.int32, (BLOCK, LANES), 0)
        bias_m = jnp.where(row >= PAD, cm_ref[...] * (-LOG2E), NEG_INF)
        km0_sc[...], km1_sc[...], vm0_sc[...], vm1_sc[...] = prep(km_ref[...], vm_ref[...], bias_m)

        def prep_block(i, carry):
            rows = pl.ds(pl.multiple_of(i * tq, tq), tq)
            k0_sc[i], k1_sc[i], v0_sc[i], v1_sc[i] = prep(kt_ref[rows, :], vt_ref[rows, :],
                                                         ct_ref[rows, :] * (-LOG2E))
            return carry

        lax.fori_loop(0, k0_sc.shape[0], prep_block, 0)

    q = q_ref[...].astype(F32)
    ones0 = jnp.where((lane >= extra0) & (lane < extra0 + _BIAS_PIECES), 1.0, 0.0)
    ones1 = jnp.where((lane >= extra1) & (lane < extra1 + _BIAS_PIECES), 1.0, 0.0)
    q0 = jnp.where(is0, q, ones0).astype(BF16)
    q1 = jnp.where(is0, ones1, q).astype(BF16)

    def consume(m, acc, s, vt, mask):
        if mask is not None:
            s = jnp.where(mask, s, NEG_INF)
        m_new = jnp.maximum(m, jnp.max(s, axis=0, keepdims=True))
        p = jnp.exp2(s - m_new).astype(BF16)
        return m_new, acc * jnp.exp2(m - m_new) + _dot(vt, p)

    neg = jnp.full((1, tq), NEG_INF, F32)
    zero = jnp.zeros((LANES, tq), F32)
    m0, acc0 = consume(neg, zero, _dot_nt(km0_sc[...], q0), vm0_sc[...], None)
    m1, acc1 = consume(neg, zero, _dot_nt(km1_sc[...], q1), vm1_sc[...], None)

    def step(j, carry):
        m0, acc0, m1, acc1, s0, s1 = carry
        s0_next = _dot_nt(k0_sc[j + 1], q0)
        s1_next = _dot_nt(k1_sc[j + 1], q1)
        m0, acc0 = consume(m0, acc0, s0, v0_sc[j], None)
        m1, acc1 = consume(m1, acc1, s1, v1_sc[j], None)
        return m0, acc0, m1, acc1, s0_next, s1_next

    carry = (m0, acc0, m1, acc1, _dot_nt(k0_sc[0], q0), _dot_nt(k1_sc[0], q1))
    m0, acc0, m1, acc1, s0, s1 = lax.fori_loop(0, qi, step, carry)
    causal = lax.broadcasted_iota(jnp.int32, (tq, 1), 0) <= lax.broadcasted_iota(jnp.int32, (1, tq), 1)
    _, acc0 = consume(m0, acc0, s0, v0_sc[qi], causal)
    _, acc1 = consume(m1, acc1, s1, v1_sc[qi], causal)
    top = lax.broadcasted_iota(jnp.int32, (LANES, 1), 0) < FOX_HEAD_DIM
    o_t = jnp.where(top, acc0 / acc0[extra0:extra0 + 1, :], acc1 / acc1[extra1:extra1 + 1, :])
    eye_q = (lax.broadcasted_iota(jnp.int32, (tq, tq), 0) == lax.broadcasted_iota(jnp.int32, (tq, tq), 1)).astype(BF16)
    o_ref[...] = _dot_nt(eye_q, o_t.astype(BF16)).astype(BF16)


def _fox_prompt(q, k, v, c_tok, k_small, v_small, c_meta, meta_blk, batch, seq, tq):
    nq = seq // tq
    npair = FOX_INNER // LANES
    kv_spec = pl.BlockSpec((seq, LANES), lambda b, hp, i: (b, hp))
    meta_spec = pl.BlockSpec((BLOCK, LANES), lambda b, hp, i: (meta_blk, hp))
    blk_spec = pl.BlockSpec((tq, LANES), lambda b, hp, i: (b * nq + i, hp))
    return pl.pallas_call(
        functools.partial(_fox_prompt_body, tq=tq),
        grid=(batch, npair, nq),
        in_specs=[blk_spec, kv_spec, kv_spec, pl.BlockSpec((seq, LANES), lambda b, hp, i: (b, 0)),
                  meta_spec, meta_spec, pl.BlockSpec((None, BLOCK, LANES), lambda b, hp, i: (b, 0, 0))],
        out_specs=blk_spec,
        out_shape=jax.ShapeDtypeStruct((batch * seq, FOX_INNER), BF16),
        scratch_shapes=[pltpu.VMEM((nq, tq, LANES), BF16)] * 2 + [pltpu.VMEM((nq, LANES, tq), BF16)] * 2
                       + [pltpu.VMEM((BLOCK, LANES), BF16)] * 4,
        compiler_params=_params(3, 48),
        name="fox_prompt",
    )(q, k, v, c_tok, k_small, v_small, c_meta)


def _ssd_chunk(xbuf_ref, dt, z, st_ref, ybuf_ref, cw_ref, cb_ref, a_ref, dskip_ref, nw_ref, wout_ref):
    L = BLOCK
    conv = cb_ref[...]
    for w in range(CONV_WIDTH):
        conv = conv + xbuf_ref[HIST - (CONV_WIDTH - 1) + w:HIST - (CONV_WIDTH - 1) + w + L, :] * cw_ref[w:w + 1, :]
    xc = _silu(conv)
    xs = xc[:, :SSD_INNER]
    gn = SSD_GROUPS * SSD_STATE

    a_col = _cumsum_rows(dt * a_ref[...])
    a_row = _transpose_exact(a_col)
    dt_row = _transpose_exact(dt)
    lane_head = lax.broadcasted_iota(jnp.int32, (LANES, SSD_HEADS * LANES), 1) // LANES
    e_wide = (lane_head == lax.broadcasted_iota(jnp.int32, (LANES, SSD_HEADS * LANES), 0)).astype(BF16)
    a_bc = _dot_exact_r(a_col, e_wide)
    col_head = lax.broadcasted_iota(jnp.int32, (LANES, SSD_INNER), 1) // SSD_HEAD_DIM
    e_head = (col_head == lax.broadcasted_iota(jnp.int32, (LANES, SSD_INNER), 0)).astype(BF16)
    a_exp = _dot_exact_r(a_col, e_head)
    decay_in = jnp.exp(a_exp)
    a_last = a_exp[L - 1:L, :]
    to_end = jnp.exp(a_last - a_exp) * _dot_exact_r(dt, e_head)
    xw = (xs * to_end).astype(BF16)
    chunk_decay = decay_in[L - 1:L, :]

    causal = lax.broadcasted_iota(jnp.int32, (L, L), 1) <= lax.broadcasted_iota(jnp.int32, (L, L), 0)
    is0 = lax.broadcasted_iota(jnp.int32, (1, LANES), 1) < SSD_HEAD_DIM
    heads_per_group = SSD_HEADS // SSD_GROUPS
    for g in range(SSD_GROUPS):
        b_g = xc[:, SSD_INNER + g * SSD_STATE:SSD_INNER + (g + 1) * SSD_STATE].astype(BF16)
        c_g = xc[:, SSD_INNER + gn + g * SSD_STATE:SSD_INNER + gn + (g + 1) * SSD_STATE].astype(BF16)
        cb = _dot_nt(c_g, b_g)
        lo, hi = g * GROUP_W, (g + 1) * GROUP_W
        h_in = st_ref[g]
        y_off = _dot(c_g, h_in.astype(BF16)) * decay_in[:, lo:hi]
        ybuf_ref[:, lo:hi] = y_off
        eye = (lax.broadcasted_iota(jnp.int32, (L, L), 0) == lax.broadcasted_iota(jnp.int32, (L, L), 1)).astype(BF16)
        b_t = _dot_nt(eye, b_g).astype(BF16)
        st_ref[g] = h_in * chunk_decay[:, lo:hi] + _dot(b_t, xw[:, lo:hi])
        for pair in range(heads_per_group // 2):
            h0 = g * heads_per_group + 2 * pair
            c0 = h0 * SSD_HEAD_DIM
            x_pair = xs[:, c0:c0 + LANES]
            y_pair = jnp.zeros((L, LANES), F32)
            for k, xm in ((0, jnp.where(is0, x_pair, 0.0)), (1, jnp.where(is0, 0.0, x_pair))):
                h = h0 + k
                seg = a_bc[:, h * LANES:(h + 1) * LANES] - a_row[h:h + 1, :]
                wgt = cb * jnp.exp(jnp.where(causal, seg, -jnp.inf)) * dt_row[h:h + 1, :]
                y_pair = y_pair + _dot(wgt.astype(BF16), xm.astype(BF16))
            ybuf_ref[:, c0:c0 + LANES] += y_pair

    y = (ybuf_ref[...] + xs * dskip_ref[...]) * _silu(z)
    yn = (_rms(y) * nw_ref[...]).astype(BF16)
    return _dot(yn, wout_ref[...])


_SSD_WEIGHT_SHAPES = ((SUBLANES, CONV_DIM), (1, CONV_DIM), (1, LANES), (1, SSD_INNER), (1, SSD_INNER),
                      (SSD_INNER, D_MODEL))


def _ssd_prompt_body(xt_ref, zt_ref, st_ref_in, xm_ref, sm_ref, cw_ref, cb_ref, a_ref, dskip_ref, nw_ref, wout_ref,
                     o_ref, state_ref, xbuf_ref, st_ref, ybuf_ref):
    c = pl.program_id(1)
    lane = lax.broadcasted_iota(jnp.int32, (BLOCK, LANES), 1)
    row = lax.broadcasted_iota(jnp.int32, (BLOCK, LANES), 0)

    @pl.when(c == 0)
    def _():
        st_ref[...] = jnp.zeros_like(st_ref)
        xbuf_ref[0:HIST, :] = jnp.zeros((HIST, CONV_DIM), F32)
        xbuf_ref[HIST:, :] = xm_ref[...]

    @pl.when(c > 0)
    def _():
        xbuf_ref[HIST:, :] = xt_ref[...]

    dt = jnp.where(c == 0,
                   jnp.where((lane < SSD_HEADS) & (row >= PAD), sm_ref[...], 0.0),
                   jnp.where(lane < SSD_HEADS, st_ref_in[...], 0.0))
    o_ref[...] = _ssd_chunk(xbuf_ref, dt, zt_ref[...], st_ref, ybuf_ref, cw_ref, cb_ref, a_ref, dskip_ref, nw_ref,
                            wout_ref)
    xbuf_ref[0:HIST, :] = xbuf_ref[BLOCK:BLOCK + HIST, :]

    @pl.when(c == pl.num_programs(1) - 1)
    def _():
        state_ref[...] = st_ref[...]


def _ssd_prompt(xbc, z, sm, xbc_small, sm_small, meta_blk, weights, batch, seq):
    nc = seq // BLOCK
    tok = lambda w: pl.BlockSpec((BLOCK, w), lambda b, c: (b * nc + jnp.maximum(c - 1, 0), 0))
    meta = lambda w: pl.BlockSpec((BLOCK, w), lambda b, c: (meta_blk, 0))
    return pl.pallas_call(
        _ssd_prompt_body,
        grid=(batch, nc + 1),
        in_specs=[tok(CONV_DIM), tok(SSD_INNER), tok(LANES), meta(CONV_DIM), meta(LANES)]
                 + [_resident(s) for s in _SSD_WEIGHT_SHAPES],
        out_specs=[tok(D_MODEL),
                   pl.BlockSpec((None, SSD_GROUPS, SSD_STATE, GROUP_W), lambda b, c: (b, 0, 0, 0))],
        out_shape=[jax.ShapeDtypeStruct((batch * seq, D_MODEL), F32),
                   jax.ShapeDtypeStruct((batch, SSD_GROUPS, SSD_STATE, GROUP_W), F32)],
        scratch_shapes=[pltpu.VMEM((HIST + BLOCK, CONV_DIM), F32),
                        pltpu.VMEM((SSD_GROUPS, SSD_STATE, GROUP_W), F32),
                        pltpu.VMEM((BLOCK, SSD_INNER), F32)],
        compiler_params=_params(2, 40),
        name="ssd_prompt",
    )(xbc, z, sm, xbc_small, sm_small, *weights)


def _ssd_sample_body(x_ref, z_ref, sm_ref, hist_ref, state0_ref, cw_ref, cb_ref, a_ref, dskip_ref, nw_ref, wout_ref,
                     o_ref, state_ref, xbuf_ref, st_ref, ybuf_ref, dt_ref, zbuf_ref, *, n_new):
    rows = x_ref.shape[0]
    xbuf_ref[0:HIST, :] = hist_ref[...]
    xbuf_ref[HIST:HIST + rows, :] = x_ref[...]
    xbuf_ref[HIST + rows:, :] = jnp.zeros((BLOCK - rows, CONV_DIM), F32)
    lane = lax.broadcasted_iota(jnp.int32, (rows, LANES), 1)
    row = lax.broadcasted_iota(jnp.int32, (rows, LANES), 0)
    dt_ref[...] = jnp.zeros_like(dt_ref)
    dt_ref[0:rows, :] = jnp.where((lane < SSD_HEADS) & (row < n_new), sm_ref[...], 0.0)
    zbuf_ref[...] = jnp.zeros_like(zbuf_ref)
    zbuf_ref[0:rows, :] = z_ref[...]
    st_ref[...] = state0_ref[...]
    out = _ssd_chunk(xbuf_ref, dt_ref[...], zbuf_ref[...], st_ref, ybuf_ref, cw_ref, cb_ref, a_ref, dskip_ref, nw_ref,
                     wout_ref)
    o_ref[...] = out[0:rows, :]
    state_ref[...] = st_ref[...]


def _ssd_sample(x8, z8, sm8, hist8, state0, weights, n_new):
    nb, rows = x8.shape[0], x8.shape[1]
    per = lambda w: pl.BlockSpec((None, rows, w), lambda b: (b, 0, 0))
    st_spec = pl.BlockSpec((None, SSD_GROUPS, SSD_STATE, GROUP_W), lambda b: (b, 0, 0, 0))
    return pl.pallas_call(
        functools.partial(_ssd_sample_body, n_new=n_new),
        grid=(nb,),
        in_specs=[per(CONV_DIM), per(SSD_INNER), per(LANES), per(CONV_DIM), st_spec]
                 + [_resident(s) for s in _SSD_WEIGHT_SHAPES],
        out_specs=[per(D_MODEL), st_spec],
        out_shape=[jax.ShapeDtypeStruct((nb, rows, D_MODEL), F32),
                   jax.ShapeDtypeStruct((nb, SSD_GROUPS, SSD_STATE, GROUP_W), F32)],
        scratch_shapes=[pltpu.VMEM((HIST + BLOCK, CONV_DIM), F32),
                        pltpu.VMEM((SSD_GROUPS, SSD_STATE, GROUP_W), F32),
                        pltpu.VMEM((BLOCK, SSD_INNER), F32),
                        pltpu.VMEM((BLOCK, LANES), F32),
                        pltpu.VMEM((BLOCK, SSD_INNER), F32)],
        compiler_params=_params(1, 40),
        name="ssd_sample",
    )(x8, z8, sm8, hist8, state0, *weights)


def _page_cumsum_body(x_ref, c_ref, tot_ref):
    x = x_ref[...]
    c_ref[...] = _dot_exact_r(x, _triu(LANES))
    tot_ref[...] = _dot_exact_r(x, jnp.ones((LANES, LANES), BF16))


def _page_cumsum(lf_t, tr):
    rows = lf_t.shape[0]
    spec = pl.BlockSpec((tr, LANES), lambda i: (i, 0))
    return pl.pallas_call(
        _page_cumsum_body,
        grid=(rows // tr,),
        in_specs=[spec],
        out_specs=[spec, spec],
        out_shape=[jax.ShapeDtypeStruct((rows, LANES), F32)] * 2,
        compiler_params=_params(1, 32),
        name="page_cumsum",
    )(lf_t)


def _fox_sample_body(pt_ref, q_ref, kn_ref, vn_ref, ln_ref, *refs, pps, n_new):
    k_refs, v_refs, c_refs, t_refs = refs[:pps], refs[pps:2 * pps], refs[2 * pps:3 * pps], refs[3 * pps:4 * pps]
    o_ref, qbd_sc, m_sc, l_sc, acc_sc, carry_sc, kn_sc, vn_sc = refs[4 * pps:]
    del pt_ref
    j = pl.program_id(1)
    nrow = n_new * FOX_HEADS

    @pl.when(j == 0)
    def _():
        col_head = lax.broadcasted_iota(jnp.int32, (FOX_HEADS, FOX_INNER), 1) // FOX_HEAD_DIM
        own = col_head == lax.broadcasted_iota(jnp.int32, (FOX_HEADS, FOX_INNER), 0)
        q = q_ref[...].astype(F32)
        for t in range(n_new):
            qbd_sc[t * FOX_HEADS:(t + 1) * FOX_HEADS, :] = jnp.where(own, q[t:t + 1, :], 0.0).astype(BF16)
        m_sc[...] = jnp.full_like(m_sc, NEG_INF)
        l_sc[...] = jnp.zeros_like(l_sc)
        acc_sc[...] = jnp.zeros_like(acc_sc)
        carry_sc[...] = jnp.zeros_like(carry_sc)

    qbd = qbd_sc[...]

    def tile_rows(bias16):
        return jnp.concatenate([bias16] * n_new, axis=0)

    def accumulate(s_list, pv_fn_list):
        m_old = m_sc[...]
        m_new = m_old
        for s in s_list:
            m_new = jnp.maximum(m_new, jnp.max(s, axis=1, keepdims=True))
        alpha = jnp.exp2(m_old - m_new)
        l_new = alpha * l_sc[...]
        pv = jnp.zeros(acc_sc.shape, F32)
        for s, pv_fn in zip(s_list, pv_fn_list):
            p = jnp.exp2(s - m_new)
            l_new = l_new + jnp.sum(p, axis=1, keepdims=True)
            pv = pv + pv_fn(p.astype(BF16))
        m_sc[...] = m_new
        l_sc[...] = l_new
        acc_sc[...] = acc_sc[...] * alpha + pv

    carry = carry_sc[...]
    s_list, pv_fns = [], []
    for i in range(pps):
        kt = k_refs[i][...].reshape(FOX_INNER, BLOCK).astype(BF16)
        vt = v_refs[i][...].reshape(FOX_INNER, BLOCK).astype(BF16)
        s_list.append(_dot(qbd, kt) - tile_rows((carry + c_refs[i][...]) * LOG2E))
        pv_fns.append(functools.partial(lambda p, vt: _dot_nt(p, vt), vt=vt))
        carry = carry + t_refs[i][...]
    carry_sc[...] = carry
    accumulate(s_list, pv_fns)

    @pl.when(j == pl.num_programs(1) - 1)
    def _():
        rows = kn_ref.shape[0]
        kn_sc[...] = jnp.zeros_like(kn_sc)
        vn_sc[...] = jnp.zeros_like(vn_sc)
        kn_sc[0:rows, :] = kn_ref[...].astype(BF16)
        vn_sc[0:rows, :] = vn_ref[...].astype(BF16)
        bias = (carry_sc[...] + _dot_exact_r(ln_ref[...], _triu(LANES))) * LOG2E
        s = _dot_nt(qbd, kn_sc[...]) - tile_rows(bias)
        key = lax.broadcasted_iota(jnp.int32, (nrow, BLOCK), 1)
        tok = lax.broadcasted_iota(jnp.int32, (nrow, BLOCK), 0) // FOX_HEADS
        s = jnp.where(key <= tok, s, NEG_INF)
        vn = vn_sc[...]
        accumulate([s], [lambda p: _dot(p, vn)])
        o = acc_sc[...] / l_sc[...]
        row_head = lax.broadcasted_iota(jnp.int32, (nrow, FOX_INNER), 0) % FOX_HEADS
        col_head = lax.broadcasted_iota(jnp.int32, (nrow, FOX_INNER), 1) // FOX_HEAD_DIM
        o = jnp.where(row_head == col_head, o, 0.0)
        o_ref[...] = jnp.sum(o.reshape(n_new, FOX_HEADS, FOX_INNER), axis=1)


def _fox_sample(page_table, q, k_new8, v_new8, lf_new_t, cache_kt, cache_vt, c_page, tot_page, pps):
    nb, n_new = q.shape[0], q.shape[1]
    n_pages = page_table.shape[1]
    rows8 = k_new8.shape[1]
    pt_flat = page_table.reshape(-1)

    def page_idx(i):
        return lambda b, j, pt: pt[b * n_pages + j * pps + i]

    def kv_spec(i):
        f = page_idx(i)
        return pl.BlockSpec((None, None, FOX_HEADS, FOX_HEAD_DIM, BLOCK), lambda b, j, pt: (0, f(b, j, pt), 0, 0, 0))

    def c_spec(i):
        f = page_idx(i)
        return pl.BlockSpec((None, FOX_HEADS, LANES), lambda b, j, pt: (f(b, j, pt), 0, 0))

    per = lambda r, w: pl.BlockSpec((None, r, w), lambda b, j, pt: (b, 0, 0))
    in_specs = [per(n_new, FOX_INNER), per(rows8, FOX_INNER), per(rows8, FOX_INNER), per(FOX_HEADS, LANES)]
    in_specs += [kv_spec(i) for i in range(pps)] * 2 + [c_spec(i) for i in range(pps)] * 2
    nrow = n_new * FOX_HEADS
    grid_spec = pltpu.PrefetchScalarGridSpec(
        num_scalar_prefetch=1,
        grid=(nb, n_pages // pps),
        in_specs=in_specs,
        out_specs=per(n_new, FOX_INNER),
        scratch_shapes=[pltpu.VMEM((nrow, FOX_INNER), BF16), pltpu.VMEM((nrow, 1), F32), pltpu.VMEM((nrow, 1), F32),
                        pltpu.VMEM((nrow, FOX_INNER), F32), pltpu.VMEM((FOX_HEADS, LANES), F32),
                        pltpu.VMEM((BLOCK, FOX_INNER), BF16), pltpu.VMEM((BLOCK, FOX_INNER), BF16)],
    )
    return pl.pallas_call(
        functools.partial(_fox_sample_body, pps=pps, n_new=n_new),
        grid_spec=grid_spec,
        out_shape=jax.ShapeDtypeStruct((nb, n_new, FOX_INNER), F32),
        compiler_params=_params(2, 56),
        name="fox_sample",
    )(pt_flat, q, k_new8, v_new8, lf_new_t, *([cache_kt] * pps), *([cache_vt] * pps), *([c_page] * pps),
      *([tot_page] * pps))


def _ffn_weights(w_gate, w_up, w_down):
    return w_gate.astype(BF16), w_up.astype(BF16), w_down.astype(BF16)


def _pick_tile(n, candidates):
    for t in candidates:
        if n % t == 0:
            return t
    raise ValueError(f"no tile for {n}")


def kernel(x_prompt, x_sample, cache_k, cache_v, cache_logf, page_table, state_conv, state_ssm, meta_tokens, norm_ffn1, w_ffn1_gate, w_ffn1_up, w_ffn1_down, norm_mix, w_in, conv_w, conv_b, dt_bias, A_log, D_skip, ssd_norm, w_ssd_out, b_forget, w_fox_out, w_o, norm_ffn2, w_ffn2_gate, w_ffn2_up, w_ffn2_down, norm_final):
    batch, seq, _ = x_prompt.shape
    nb, n_new, _ = x_sample.shape
    depth, n_phys, page = cache_k.shape[0], cache_k.shape[1], cache_k.shape[2]
    assert depth == 1 and page == BLOCK and seq % BLOCK == 0 and n_new <= SUBLANES
    n_s = nb * n_new
    n_s_pad = -(-n_s // BLOCK) * BLOCK
    meta_blk = n_s_pad // BLOCK
    rows_p = batch * seq

    ffn1_w = _ffn_weights(w_ffn1_gate[0], w_ffn1_up[0], w_ffn1_down[0])
    ffn2_w = _ffn_weights(w_ffn2_gate[0], w_ffn2_up[0], w_ffn2_down[0])
    wz, wxbc, wdt, wq, wk, wv, wf, wgs, wgf = jnp.split(
        w_in[0], [1024, 2560, 2576, 3600, 4624, 5648, 5664, 6688], axis=1)
    w_main = jnp.concatenate([wz, wxbc, wq, wk, wv, wgs, wgf], axis=1).astype(BF16)
    w_small = jnp.pad(jnp.concatenate([wdt, wf], axis=1), ((0, 0), (0, LANES - SSD_HEADS - FOX_HEADS))).astype(BF16)
    bias_small = jnp.pad(jnp.concatenate([dt_bias[0], b_forget[0]]), (0, LANES - SSD_HEADS - FOX_HEADS))[None, :]
    row = lambda a: a.reshape(1, -1).astype(F32)
    ssd_w = (jnp.pad(conv_w[0], ((0, SUBLANES - CONV_WIDTH), (0, 0))), row(conv_b[0]),
             jnp.pad(-jnp.exp(A_log[0].astype(F32)), (0, LANES - SSD_HEADS))[None, :],
             row(jnp.repeat(D_skip[0], SSD_HEAD_DIM)), row(ssd_norm[0]), w_ssd_out[0].astype(BF16))
    w_fox_b, w_o_b = w_fox_out[0].astype(BF16), w_o[0].astype(BF16)

    xp = x_prompt.reshape(rows_p, D_MODEL)
    x_small = jnp.concatenate([x_sample.reshape(n_s, D_MODEL), jnp.zeros((n_s_pad - n_s + PAD, D_MODEL), F32),
                               meta_tokens.astype(F32)], axis=0)
    rows_s = n_s_pad + BLOCK

    tm_ffn = _pick_tile(rows_p, (512, 256, 128))
    tm_in = _pick_tile(rows_p, (256, 128))

    x1 = _ffn(xp, rows_p, tm_ffn, row(norm_ffn1[0]), *ffn1_w, name="ffn1")
    x1s = _ffn(x_small, rows_s, BLOCK, row(norm_ffn1[0]), *ffn1_w, name="ffn1_small")
    z, xbc, q, k, v, gs, gf, sm = _inproj(x1, tm_in, row(norm_mix[0]), w_main, w_small, bias_small, "inproj")
    zs, xbcs, qs, ks, vs, gss, gfs, sms = _inproj(x1s, BLOCK, row(norm_mix[0]), w_main, w_small, bias_small,
                                                  "inproj_small")

    o_ssd, st_p = _ssd_prompt(xbc, z, sm, xbcs, sms, meta_blk, ssd_w, batch, seq)

    tq = _pick_tile(seq, (512, 256, 128))
    c_tok, c_meta = _logf_cumsum(sm, sms, meta_blk, batch, seq, _pick_tile(seq, (512, 256, 128)))
    lo, hi = SSD_HEADS, SSD_HEADS + FOX_HEADS
    o_att = _fox_prompt(q, k, v, c_tok, ks, vs, c_meta, meta_blk, batch, seq, tq)

    y_p = _ffn(x1, rows_p, tm_ffn, row(norm_ffn2[0]), *ffn2_w, merge_args=(o_ssd, o_att, gs, gf, w_fox_b, w_o_b),
               final_norm_w=row(norm_final), name="merge_ffn2")

    pad8 = lambda a: jnp.pad(a[:n_s].reshape(nb, n_new, -1), ((0, 0), (0, SUBLANES - n_new), (0, 0)))
    hist8 = jnp.pad(state_conv[0].astype(F32), ((0, 0), (SUBLANES - (CONV_WIDTH - 1), 0), (0, 0)))
    rper = SSD_HEADS // SSD_GROUPS
    state0 = state_ssm[0].astype(F32).reshape(nb, SSD_GROUPS, rper, SSD_HEAD_DIM, SSD_STATE)
    state0 = state0.transpose(0, 1, 4, 2, 3).reshape(nb, SSD_GROUPS, SSD_STATE, GROUP_W)
    o_ssd_s8, st_s = _ssd_sample(pad8(xbcs), pad8(zs), pad8(sms), hist8, state0, ssd_w, n_new)

    lf_t = cache_logf[0].astype(F32).transpose(0, 2, 1).reshape(n_phys * FOX_HEADS, page)
    c_page, tot_page = _page_cumsum(lf_t, _pick_tile(n_phys * FOX_HEADS, (4096, 2048, 1024, 512, 256, 128, 16)))
    c_page = c_page.reshape(n_phys, FOX_HEADS, page)
    tot_page = tot_page.reshape(n_phys, FOX_HEADS, page)
    lf_new_t = jnp.pad(sms[:n_s, lo:hi].reshape(nb, n_new, FOX_HEADS).transpose(0, 2, 1),
                       ((0, 0), (0, 0), (0, LANES - n_new)))
    pps = _pick_tile(page_table.shape[1], (16, 8, 4, 2, 1))
    o_att_s = _fox_sample(page_table, qs[:n_s].reshape(nb, n_new, FOX_INNER), pad8(ks), pad8(vs), lf_new_t,
                          cache_k.transpose(0, 1, 3, 4, 2), cache_v.transpose(0, 1, 3, 4, 2), c_page, tot_page, pps)

    rows_pad = lambda a: jnp.pad(a.reshape(n_s, -1), ((0, n_s_pad - n_s), (0, 0)))
    y_s = _ffn(x1s, n_s_pad, BLOCK, row(norm_ffn2[0]), *ffn2_w,
               merge_args=(rows_pad(o_ssd_s8[:, :n_new]), rows_pad(o_att_s).astype(BF16), gss, gfs, w_fox_b, w_o_b),
               final_norm_w=row(norm_final), name="merge_ffn2_small")

    meta_rows = slice(n_s_pad + PAD, n_s_pad + BLOCK)

    def with_meta(tok, small, width):
        m = jnp.broadcast_to(small[meta_rows][None], (batch, N_META, width))
        return jnp.concatenate([m, tok.reshape(batch, seq, width)], axis=1)

    def unpack_state(st, n):
        st = st.reshape(n, SSD_GROUPS, SSD_STATE, rper, SSD_HEAD_DIM).transpose(0, 1, 3, 4, 2)
        return st.reshape(1, n, SSD_HEADS, SSD_HEAD_DIM, SSD_STATE)

    y_prompt = y_p.reshape(batch, seq, D_MODEL)
    y_sample = y_s[:n_s].reshape(nb, n_new, D_MODEL)
    k_p = with_meta(k, ks, FOX_INNER).reshape(1, batch, seq + N_META, FOX_HEADS, FOX_HEAD_DIM)
    v_p = with_meta(v, vs, FOX_INNER).reshape(1, batch, seq + N_META, FOX_HEADS, FOX_HEAD_DIM)
    lf_p = with_meta(sm[:, lo:hi], sms[:, lo:hi], FOX_HEADS)[None]
    conv_p = xbc.reshape(batch, seq, CONV_DIM)[:, seq - (CONV_WIDTH - 1):][None]
    ssm_p = unpack_state(st_p, batch)
    k_s = ks[:n_s].reshape(1, nb, n_new, FOX_HEADS, FOX_HEAD_DIM)
    v_s = vs[:n_s].reshape(1, nb, n_new, FOX_HEADS, FOX_HEAD_DIM)
    lf_s = sms[:n_s, lo:hi].reshape(1, nb, n_new, FOX_HEADS)
    conv_s = jnp.concatenate([state_conv[0].astype(F32), xbcs[:n_s].reshape(nb, n_new, CONV_DIM)], axis=1)[:, n_new:][None]
    ssm_s = unpack_state(st_s, nb)
    return (y_prompt, y_sample, k_p, v_p, lf_p, conv_p, ssm_p, k_s, v_s, lf_s, conv_s, ssm_s)
```

```python
import functools

import jax
import jax.numpy as jnp
from jax import lax
from jax.experimental import pallas as pl
from jax.experimental.pallas import tpu as pltpu

F32 = jnp.float32
BF16 = jnp.bfloat16

D_MODEL = 1024
D_FF = 2816
NORM_EPS = 1e-6
N_META = 16
BLOCK = 128
PAD = BLOCK - N_META
SSD_HEADS = 16
SSD_HEAD_DIM = 64
SSD_INNER = SSD_HEADS * SSD_HEAD_DIM
SSD_STATE = 128
SSD_GROUPS = 2
GROUP_W = SSD_INNER // SSD_GROUPS
CONV_WIDTH = 4
CONV_DIM = SSD_INNER + 2 * SSD_GROUPS * SSD_STATE
FOX_HEADS = 16
FOX_HEAD_DIM = 64
FOX_INNER = FOX_HEADS * FOX_HEAD_DIM
FOX_SCALE = FOX_HEAD_DIM ** -0.5
NEG_INF = -1e30

LANES = 128
SUBLANES = 8
V7X_VMEM_BYTES = 64 * 1024 * 1024

FF_CHUNK = D_FF // 2
LOG2E = 1.4426950408889634
Q_SCALE = FOX_SCALE * LOG2E
HIST = SUBLANES


def _dot(a, b):
    return jnp.dot(a, b, preferred_element_type=F32)


def _dot_nt(a, b):
    return lax.dot_general(a, b, (((1,), (1,)), ((), ())), preferred_element_type=F32)


def _split3(x):
    hi = x.astype(BF16)
    r = x - hi.astype(F32)
    mid = r.astype(BF16)
    lo = (r - mid.astype(F32)).astype(BF16)
    return hi, mid, lo


def _dot_exact(a01, x):
    hi, mid, lo = _split3(x)
    return _dot(a01, hi) + _dot(a01, mid) + _dot(a01, lo)


def _dot_exact_r(x, b01):
    hi, mid, lo = _split3(x)
    return _dot(hi, b01) + _dot(mid, b01) + _dot(lo, b01)


def _transpose_exact(x, eye):
    hi, mid, lo = _split3(x)
    return _dot_nt(eye, hi) + _dot_nt(eye, mid) + _dot_nt(eye, lo)


def _tril(n):
    return (lax.broadcasted_iota(jnp.int32, (n, n), 1) <= lax.broadcasted_iota(jnp.int32, (n, n), 0)).astype(BF16)


def _triu(n):
    return (lax.broadcasted_iota(jnp.int32, (n, n), 0) <= lax.broadcasted_iota(jnp.int32, (n, n), 1)).astype(BF16)


def _rms(x):
    return x * lax.rsqrt(jnp.mean(x * x, axis=-1, keepdims=True) + NORM_EPS)


def _silu(x):
    return x * jax.nn.sigmoid(x)


def _params(n_axes, vmem_mib):
    return pltpu.CompilerParams(dimension_semantics=("arbitrary",) * n_axes,
                                vmem_limit_bytes=min(vmem_mib * 1024 * 1024, V7X_VMEM_BYTES - 4 * 1024 * 1024))


def _resident(shape):
    nd = len(shape)
    return pl.BlockSpec(shape, lambda *_: (0,) * nd, pipeline_mode=pl.Buffered(1))


def _ffn_body(*refs, merge, final_norm):
    refs = list(refs)
    x_ref = refs.pop(0)
    if merge:
        os_ref, oa_ref, gs_ref, gf_ref, wf_ref, wo_ref = refs[:6]
        refs = refs[6:]
    nw_ref, wg_ref, wu_ref, wd_ref = refs[:4]
    refs = refs[4:]
    if final_norm:
        fnw_ref = refs.pop(0)
    o_ref, hn_ref = refs

    x = x_ref[...]
    if merge:
        o_fox = _dot(oa_ref[...], wf_ref[...])
        mix = jax.nn.sigmoid(gs_ref[...]) * os_ref[...] + jax.nn.sigmoid(gf_ref[...]) * o_fox
        x = x + _dot(mix.astype(BF16), wo_ref[...])
    hn_ref[...] = (_rms(x) * nw_ref[...]).astype(BF16)
    y = x
    for c in range(D_FF // FF_CHUNK):
        cols = slice(c * FF_CHUNK, (c + 1) * FF_CHUNK)
        g = _dot(hn_ref[...], wg_ref[:, cols])
        u = _dot(hn_ref[...], wu_ref[:, cols])
        y = y + 0.5 * _dot((_silu(g) * u).astype(BF16), wd_ref[cols, :])
    if final_norm:
        y = _rms(y) * fnw_ref[...]
    o_ref[...] = y


def _ffn(x, n_rows, tm, norm_w, wg, wu, wd, merge_args=None, final_norm_w=None, name="ffn"):
    merge = merge_args is not None
    row_spec = pl.BlockSpec((tm, D_MODEL), lambda i: (i, 0))
    args, specs = [x], [row_spec]
    if merge:
        o_ssd, o_att, g_ssd, g_fox, w_fox, w_o = merge_args
        args += [o_ssd, o_att, g_ssd, g_fox, w_fox, w_o]
        specs += [row_spec] * 4 + [_resident(w_fox.shape), _resident(w_o.shape)]
    args += [norm_w, wg, wu, wd]
    specs += [_resident(norm_w.shape), _resident(wg.shape), _resident(wu.shape), _resident(wd.shape)]
    if final_norm_w is not None:
        args.append(final_norm_w)
        specs.append(_resident(final_norm_w.shape))
    return pl.pallas_call(
        functools.partial(_ffn_body, merge=merge, final_norm=final_norm_w is not None),
        grid=(n_rows // tm,),
        in_specs=specs,
        out_specs=row_spec,
        out_shape=jax.ShapeDtypeStruct((n_rows, D_MODEL), F32),
        scratch_shapes=[pltpu.VMEM((tm, D_MODEL), BF16)],
        compiler_params=_params(1, 60),
        name=name,
    )(*args)


_MAIN_WIDTHS = (SSD_INNER, CONV_DIM, FOX_INNER, FOX_INNER, FOX_INNER, D_MODEL, D_MODEL)
_Q_INDEX = 2


def _inproj_body(x_ref, nw_ref, wm_ref, ws_ref, bias_ref, *refs):
    outs, sm_ref, hn_ref = refs[:7], refs[7], refs[8]
    hn_ref[...] = (_rms(x_ref[...]) * nw_ref[...]).astype(BF16)
    off = 0
    for i, (ref, w) in enumerate(zip(outs, _MAIN_WIDTHS)):
        val = _dot(hn_ref[...], wm_ref[:, off:off + w])
        ref[...] = (val * Q_SCALE).astype(BF16) if i == _Q_INDEX else val
        off += w
    raw = _dot(hn_ref[...], ws_ref[...]) + bias_ref[...]
    t = jnp.log1p(jnp.exp(-jnp.abs(raw)))
    lane = lax.broadcasted_iota(jnp.int32, raw.shape, 1)
    sm_ref[...] = jnp.where(lane < SSD_HEADS, jnp.maximum(raw, 0.0) + t,
                            jnp.where(lane < SSD_HEADS + FOX_HEADS, jnp.minimum(raw, 0.0) - t, 0.0))


def _inproj(x, tm, norm_w, w_main, w_small, bias_small, name):
    rows = x.shape[0]
    row = lambda w: pl.BlockSpec((tm, w), lambda i: (i, 0))
    widths = _MAIN_WIDTHS + (LANES,)
    return pl.pallas_call(
        _inproj_body,
        grid=(rows // tm,),
        in_specs=[row(D_MODEL), _resident(norm_w.shape), _resident(w_main.shape), _resident(w_small.shape),
                  _resident(bias_small.shape)],
        out_specs=[row(w) for w in widths],
        out_shape=[jax.ShapeDtypeStruct((rows, w), BF16 if i == _Q_INDEX else F32) for i, w in enumerate(widths)],
        scratch_shapes=[pltpu.VMEM((tm, D_MODEL), BF16)],
        compiler_params=_params(1, 60),
        name=name,
    )(x, norm_w, w_main, w_small, bias_small)


def _cumsum_rows(x):
    return _dot_exact(_tril(x.shape[0]), x)


def _logf_cumsum_body(lt_ref, lm_ref, ct_ref, cm_ref, carry_ref):
    @pl.when(pl.program_id(1) == 0)
    def _():
        row = lax.broadcasted_iota(jnp.int32, (BLOCK, LANES), 0)
        cm = _cumsum_rows(jnp.where(row >= PAD, lm_ref[...], 0.0))
        cm_ref[...] = cm
        carry_ref[...] = cm[BLOCK - 1:BLOCK, :]

    c = _cumsum_rows(lt_ref[...]) + carry_ref[...]
    ct_ref[...] = c
    carry_ref[...] = c[c.shape[0] - 1:, :]


def _logf_cumsum(sm_tok, sm_small, meta_blk, batch, seq, tb):
    nt = seq // tb
    return pl.pallas_call(
        _logf_cumsum_body,
        grid=(batch, nt),
        in_specs=[pl.BlockSpec((tb, LANES), lambda b, j: (b * nt + j, 0)),
                  pl.BlockSpec((BLOCK, LANES), lambda b, j: (meta_blk, 0))],
        out_specs=[pl.BlockSpec((tb, LANES), lambda b, j: (b * nt + j, 0)),
                   pl.BlockSpec((None, BLOCK, LANES), lambda b, j: (b, 0, 0))],
        out_shape=[jax.ShapeDtypeStruct((batch * seq, LANES), F32),
                   jax.ShapeDtypeStruct((batch, BLOCK, LANES), F32)],
        scratch_shapes=[pltpu.VMEM((1, LANES), F32)],
        compiler_params=_params(2, 32),
        name="logf_cumsum",
    )(sm_tok, sm_small)


_BIAS_PIECES = 3
_Q_GROUP = 512


def _fox_prompt_body(q_ref, kt_ref, vt_ref, ct_ref, km_ref, vm_ref, cm_ref, o_ref,
                     k0_sc, k1_sc, v0_sc, v1_sc, km0_sc, km1_sc, vm0_sc, vm1_sc, sa_sc, sb_sc, *, tq):
    hp = pl.program_id(1)
    qi = pl.program_id(2)
    lane = lax.broadcasted_iota(jnp.int32, (1, LANES), 1)
    is0 = lane < FOX_HEAD_DIM
    extra0, extra1 = FOX_HEAD_DIM, 0

    @pl.when(qi == 0)
    def _():
        src = lax.broadcasted_iota(jnp.int32, (_BIAS_PIECES * LANES, LANES), 0)
        dst = lax.broadcasted_iota(jnp.int32, (_BIAS_PIECES * LANES, LANES), 1)
        c_lane = SSD_HEADS + 2 * hp

        def route(head, base):
            return jnp.where((src % LANES == c_lane + head) & (dst == base + src // LANES), 1.0, 0.0).astype(BF16)

        route0, route1 = route(0, extra0), route(1, extra1)
        eye = (lax.broadcasted_iota(jnp.int32, (LANES, LANES), 0)
               == lax.broadcasted_iota(jnp.int32, (LANES, LANES), 1)).astype(BF16)
        one0 = jnp.where(lane == extra0, 1.0, 0.0)
        one1 = jnp.where(lane == extra1, 1.0, 0.0)

        def prep(k, v, bias):
            pieces = jnp.concatenate(_split3(bias), axis=1)
            k0 = (jnp.where(is0, k, 0.0) + _dot(pieces, route0)).astype(BF16)
            k1 = (jnp.where(is0, 0.0, k) + _dot(pieces, route1)).astype(BF16)
            v0 = _dot_nt(eye, jnp.where(is0, v, one0).astype(BF16)).astype(BF16)
            v1 = _dot_nt(eye, jnp.where(is0, one1, v).astype(BF16)).astype(BF16)
            return k0, k1, v0, v1

        row = lax.broadcasted_iota(jnp.int32, (BLOCK, LANES), 0)
        bias_m = jnp.where(row >= PAD, cm_ref[...] * (-LOG2E), NEG_INF)
        km0_sc[...], km1_sc[...], vm0_sc[...], vm1_sc[...] = prep(km_ref[...], vm_ref[...], bias_m)

        def prep_block(i, carry):
            rows = pl.ds(pl.multiple_of(i * tq, tq), tq)
            k0_sc[i], k1_sc[i], v0_sc[i], v1_sc[i] = prep(kt_ref[rows, :], vt_ref[rows, :],
                                                         ct_ref[rows, :] * (-LOG2E))
            return carry

        lax.fori_loop(0, k0_sc.shape[0], prep_block, 0)

    q = q_ref[...].astype(F32)
    ones0 = jnp.where((lane >= extra0) & (lane < extra0 + _BIAS_PIECES), 1.0, 0.0)
    ones1 = jnp.where((lane >= extra1) & (lane < extra1 + _BIAS_PIECES), 1.0, 0.0)
    q0 = jnp.where(is0, q, ones0).astype(BF16)
    q1 = jnp.where(is0, ones1, q).astype(BF16)

    def consume(m, acc, s, vt, mask):
        if mask is not None:
            s = jnp.where(mask, s, NEG_INF)
        m_new = jnp.maximum(m, jnp.max(s, axis=0, keepdims=True))
        p = jnp.exp2(s - m_new).astype(BF16)
        return m_new, acc * jnp.exp2(m - m_new) + _dot(vt, p)

    neg = jnp.full((1, tq), NEG_INF, F32)
    zero = jnp.zeros((LANES, tq), F32)
    m0, acc0 = consume(neg, zero, _dot_nt(km0_sc[...], q0), vm0_sc[...], None)
    m1, acc1 = consume(neg, zero, _dot_nt(km1_sc[...], q1), vm1_sc[...], None)

    def issue(j, s_ref):
        s_ref[0] = _dot_nt(k0_sc[j], q0)
        s_ref[1] = _dot_nt(k1_sc[j], q1)

    def retire(j, s_ref, carry, mask=None):
        m0, acc0, m1, acc1 = carry
        m0, acc0 = consume(m0, acc0, s_ref[0], v0_sc[j], mask)
        m1, acc1 = consume(m1, acc1, s_ref[1], v1_sc[j], mask)
        return m0, acc0, m1, acc1

    def pair(i, carry):
        issue(2 * i + 1, sb_sc)
        carry = retire(2 * i, sa_sc, carry)
        issue(2 * i + 2, sa_sc)
        return retire(2 * i + 1, sb_sc, carry)

    issue(0, sa_sc)
    carry = lax.fori_loop(0, qi // 2, pair, (m0, acc0, m1, acc1))
    causal = lax.broadcasted_iota(jnp.int32, (tq, 1), 0) <= lax.broadcasted_iota(jnp.int32, (1, tq), 1)
    top = lax.broadcasted_iota(jnp.int32, (LANES, 1), 0) < FOX_HEAD_DIM
    eye_q = (lax.broadcasted_iota(jnp.int32, (tq, tq), 0) == lax.broadcasted_iota(jnp.int32, (tq, tq), 1)).astype(BF16)

    def finish(carry):
        _, acc0, _, acc1 = carry
        o_t = jnp.where(top, acc0 / acc0[extra0:extra0 + 1, :], acc1 / acc1[extra1:extra1 + 1, :])
        o_ref[...] = _dot_nt(eye_q, o_t.astype(BF16)).astype(BF16)

    @pl.when(qi % 2 == 0)
    def _():
        finish(retire(qi, sa_sc, carry, causal))

    @pl.when(qi % 2 == 1)
    def _():
        issue(qi, sb_sc)
        finish(retire(qi, sb_sc, retire(qi - 1, sa_sc, carry), causal))


def _fox_prompt(q, k, v, c_tok, k_small, v_small, c_meta, meta_blk, batch, seq, tq):
    nq = seq // tq
    npair = FOX_INNER // LANES
    kv_spec = pl.BlockSpec((seq, LANES), lambda b, hp, i: (b, hp))
    meta_spec = pl.BlockSpec((BLOCK, LANES), lambda b, hp, i: (meta_blk, hp))
    blk_spec = pl.BlockSpec((tq, LANES), lambda b, hp, i: (b * nq + i, hp))
    return pl.pallas_call(
        functools.partial(_fox_prompt_body, tq=tq),
        grid=(batch, npair, nq),
        in_specs=[blk_spec, kv_spec, kv_spec, pl.BlockSpec((seq, LANES), lambda b, hp, i: (b, 0)),
                  meta_spec, meta_spec, pl.BlockSpec((None, BLOCK, LANES), lambda b, hp, i: (b, 0, 0))],
        out_specs=blk_spec,
        out_shape=jax.ShapeDtypeStruct((batch * seq, FOX_INNER), BF16),
        scratch_shapes=[pltpu.VMEM((nq, tq, LANES), BF16)] * 2 + [pltpu.VMEM((nq, LANES, tq), BF16)] * 2
                       + [pltpu.VMEM((BLOCK, LANES), BF16)] * 4 + [pltpu.VMEM((2, tq, tq), F32)] * 2,
        compiler_params=_params(3, 48),
        name="fox_prompt",
    )(q, k, v, c_tok, k_small, v_small, c_meta)


def _ssd_chunk(xbuf_ref, dt, z, st_ref, ybuf_ref, w_refs):
    L = BLOCK
    cw_ref, cb_ref, a_ref, dskip_ref, nw_ref, wout_ref, tril_ref, eye_ref, e_wide_ref, e_head_ref = w_refs
    eye = eye_ref[...]
    conv = cb_ref[...]
    for w in range(CONV_WIDTH):
        conv = conv + xbuf_ref[HIST - (CONV_WIDTH - 1) + w:HIST - (CONV_WIDTH - 1) + w + L, :] * cw_ref[w:w + 1, :]
    xc = _silu(conv)
    xs = xc[:, :SSD_INNER]
    gn = SSD_GROUPS * SSD_STATE

    a_col = _dot_exact(tril_ref[...], dt * a_ref[...])
    a_row = _transpose_exact(a_col, eye)
    dt_row = _transpose_exact(dt, eye)
    a_bc = _dot_exact_r(a_col, e_wide_ref[...])
    e_head = e_head_ref[...]
    a_exp = _dot_exact_r(a_col, e_head)
    decay_in = jnp.exp(a_exp)
    a_last = a_exp[L - 1:L, :]
    to_end = jnp.exp(a_last - a_exp) * _dot_exact_r(dt, e_head)
    xw = (xs * to_end).astype(BF16)
    chunk_decay = decay_in[L - 1:L, :]

    causal = lax.broadcasted_iota(jnp.int32, (L, L), 1) <= lax.broadcasted_iota(jnp.int32, (L, L), 0)
    is0 = lax.broadcasted_iota(jnp.int32, (1, LANES), 1) < SSD_HEAD_DIM
    heads_per_group = SSD_HEADS // SSD_GROUPS
    for g in range(SSD_GROUPS):
        b_g = xc[:, SSD_INNER + g * SSD_STATE:SSD_INNER + (g + 1) * SSD_STATE].astype(BF16)
        c_g = xc[:, SSD_INNER + gn + g * SSD_STATE:SSD_INNER + gn + (g + 1) * SSD_STATE].astype(BF16)
        cb = _dot_nt(c_g, b_g)
        lo, hi = g * GROUP_W, (g + 1) * GROUP_W
        h_in = st_ref[g]
        y_off = _dot(c_g, h_in.astype(BF16)) * decay_in[:, lo:hi]
        ybuf_ref[:, lo:hi] = y_off
        b_t = _dot_nt(eye, b_g).astype(BF16)
        st_ref[g] = h_in * chunk_decay[:, lo:hi] + _dot(b_t, xw[:, lo:hi])
        for pair in range(heads_per_group // 2):
            h0 = g * heads_per_group + 2 * pair
            c0 = h0 * SSD_HEAD_DIM
            x_pair = xs[:, c0:c0 + LANES]
            y_pair = jnp.zeros((L, LANES), F32)
            for k, xm in ((0, jnp.where(is0, x_pair, 0.0)), (1, jnp.where(is0, 0.0, x_pair))):
                h = h0 + k
                seg = a_bc[:, h * LANES:(h + 1) * LANES] - a_row[h:h + 1, :]
                wgt = cb * jnp.exp(jnp.where(causal, seg, -jnp.inf)) * dt_row[h:h + 1, :]
                y_pair = y_pair + _dot(wgt.astype(BF16), xm.astype(BF16))
            ybuf_ref[:, c0:c0 + LANES] += y_pair

    y = (ybuf_ref[...] + xs * dskip_ref[...]) * _silu(z)
    yn = (_rms(y) * nw_ref[...]).astype(BF16)
    return _dot(yn, wout_ref[...])


_SSD_WEIGHT_SHAPES = ((SUBLANES, CONV_DIM), (1, CONV_DIM), (1, LANES), (1, SSD_INNER), (1, SSD_INNER),
                      (SSD_INNER, D_MODEL),
                      (BLOCK, BLOCK), (BLOCK, BLOCK), (LANES, SSD_HEADS * LANES), (LANES, SSD_INNER))


def _ssd_routing_matrices():
    r = jnp.arange(LANES)[:, None]
    tril = (jnp.arange(BLOCK)[None, :] <= jnp.arange(BLOCK)[:, None]).astype(BF16)
    eye = jnp.eye(BLOCK, dtype=BF16)
    e_wide = (jnp.arange(SSD_HEADS * LANES)[None, :] // LANES == r).astype(BF16)
    e_head = (jnp.arange(SSD_INNER)[None, :] // SSD_HEAD_DIM == r).astype(BF16)
    return tril, eye, e_wide, e_head


def _ssd_prompt_body(xt_ref, zt_ref, st_ref_in, xm_ref, sm_ref, *rest):
    w_refs = rest[:len(_SSD_WEIGHT_SHAPES)]
    o_ref, state_ref, xbuf_ref, st_ref, ybuf_ref = rest[len(_SSD_WEIGHT_SHAPES):]
    c = pl.program_id(1)
    lane = lax.broadcasted_iota(jnp.int32, (BLOCK, LANES), 1)
    row = lax.broadcasted_iota(jnp.int32, (BLOCK, LANES), 0)

    @pl.when(c == 0)
    def _():
        st_ref[...] = jnp.zeros_like(st_ref)
        xbuf_ref[0:HIST, :] = jnp.zeros((HIST, CONV_DIM), F32)
        xbuf_ref[HIST:, :] = xm_ref[...]

    @pl.when(c > 0)
    def _():
        xbuf_ref[HIST:, :] = xt_ref[...]

    dt = jnp.where(c == 0,
                   jnp.where((lane < SSD_HEADS) & (row >= PAD), sm_ref[...], 0.0),
                   jnp.where(lane < SSD_HEADS, st_ref_in[...], 0.0))
    o_ref[...] = _ssd_chunk(xbuf_ref, dt, zt_ref[...], st_ref, ybuf_ref, w_refs)
    xbuf_ref[0:HIST, :] = xbuf_ref[BLOCK:BLOCK + HIST, :]

    @pl.when(c == pl.num_programs(1) - 1)
    def _():
        state_ref[...] = st_ref[...]


def _ssd_prompt(xbc, z, sm, xbc_small, sm_small, meta_blk, weights, batch, seq):
    nc = seq // BLOCK
    tok = lambda w: pl.BlockSpec((BLOCK, w), lambda b, c: (b * nc + jnp.maximum(c - 1, 0), 0))
    meta = lambda w: pl.BlockSpec((BLOCK, w), lambda b, c: (meta_blk, 0))
    return pl.pallas_call(
        _ssd_prompt_body,
        grid=(batch, nc + 1),
        in_specs=[tok(CONV_DIM), tok(SSD_INNER), tok(LANES), meta(CONV_DIM), meta(LANES)]
                 + [_resident(s) for s in _SSD_WEIGHT_SHAPES],
        out_specs=[tok(D_MODEL),
                   pl.BlockSpec((None, SSD_GROUPS, SSD_STATE, GROUP_W), lambda b, c: (b, 0, 0, 0))],
        out_shape=[jax.ShapeDtypeStruct((batch * seq, D_MODEL), F32),
                   jax.ShapeDtypeStruct((batch, SSD_GROUPS, SSD_STATE, GROUP_W), F32)],
        scratch_shapes=[pltpu.VMEM((HIST + BLOCK, CONV_DIM), F32),
                        pltpu.VMEM((SSD_GROUPS, SSD_STATE, GROUP_W), F32),
                        pltpu.VMEM((BLOCK, SSD_INNER), F32)],
        compiler_params=_params(2, 40),
        name="ssd_prompt",
    )(xbc, z, sm, xbc_small, sm_small, *weights)


def _ssd_sample_body(x_ref, z_ref, sm_ref, hist_ref, state0_ref, *rest, n_new):
    w_refs = rest[:len(_SSD_WEIGHT_SHAPES)]
    o_ref, state_ref, xbuf_ref, st_ref, ybuf_ref, dt_ref, zbuf_ref = rest[len(_SSD_WEIGHT_SHAPES):]
    rows = x_ref.shape[0]
    xbuf_ref[0:HIST, :] = hist_ref[...]
    xbuf_ref[HIST:HIST + rows, :] = x_ref[...]
    xbuf_ref[HIST + rows:, :] = jnp.zeros((BLOCK - rows, CONV_DIM), F32)
    lane = lax.broadcasted_iota(jnp.int32, (rows, LANES), 1)
    row = lax.broadcasted_iota(jnp.int32, (rows, LANES), 0)
    dt_ref[...] = jnp.zeros_like(dt_ref)
    dt_ref[0:rows, :] = jnp.where((lane < SSD_HEADS) & (row < n_new), sm_ref[...], 0.0)
    zbuf_ref[...] = jnp.zeros_like(zbuf_ref)
    zbuf_ref[0:rows, :] = z_ref[...]
    st_ref[...] = state0_ref[...]
    out = _ssd_chunk(xbuf_ref, dt_ref[...], zbuf_ref[...], st_ref, ybuf_ref, w_refs)
    o_ref[...] = out[0:rows, :]
    state_ref[...] = st_ref[...]


def _ssd_sample(x8, z8, sm8, hist8, state0, weights, n_new):
    nb, rows = x8.shape[0], x8.shape[1]
    per = lambda w: pl.BlockSpec((None, rows, w), lambda b: (b, 0, 0))
    st_spec = pl.BlockSpec((None, SSD_GROUPS, SSD_STATE, GROUP_W), lambda b: (b, 0, 0, 0))
    return pl.pallas_call(
        functools.partial(_ssd_sample_body, n_new=n_new),
        grid=(nb,),
        in_specs=[per(CONV_DIM), per(SSD_INNER), per(LANES), per(CONV_DIM), st_spec]
                 + [_resident(s) for s in _SSD_WEIGHT_SHAPES],
        out_specs=[per(D_MODEL), st_spec],
        out_shape=[jax.ShapeDtypeStruct((nb, rows, D_MODEL), F32),
                   jax.ShapeDtypeStruct((nb, SSD_GROUPS, SSD_STATE, GROUP_W), F32)],
        scratch_shapes=[pltpu.VMEM((HIST + BLOCK, CONV_DIM), F32),
                        pltpu.VMEM((SSD_GROUPS, SSD_STATE, GROUP_W), F32),
                        pltpu.VMEM((BLOCK, SSD_INNER), F32),
                        pltpu.VMEM((BLOCK, LANES), F32),
                        pltpu.VMEM((BLOCK, SSD_INNER), F32)],
        compiler_params=_params(1, 40),
        name="ssd_sample",
    )(x8, z8, sm8, hist8, state0, *weights)


def _page_cumsum_body(x_ref, c_ref, tot_ref):
    x = x_ref[...]
    c_ref[...] = _dot_exact_r(x, _triu(LANES))
    tot_ref[...] = _dot_exact_r(x, jnp.ones((LANES, LANES), BF16))


def _page_cumsum(lf_t, tr):
    rows = lf_t.shape[0]
    spec = pl.BlockSpec((tr, LANES), lambda i: (i, 0))
    return pl.pallas_call(
        _page_cumsum_body,
        grid=(rows // tr,),
        in_specs=[spec],
        out_specs=[spec, spec],
        out_shape=[jax.ShapeDtypeStruct((rows, LANES), F32)] * 2,
        compiler_params=_params(1, 32),
        name="page_cumsum",
    )(lf_t)


def _fox_sample_body(pt_ref, q_ref, kn_ref, vn_ref, ln_ref, *refs, pps, n_new):
    k_refs, v_refs, c_refs, t_refs = refs[:pps], refs[pps:2 * pps], refs[2 * pps:3 * pps], refs[3 * pps:4 * pps]
    o_ref, qbd_sc, m_sc, l_sc, acc_sc, carry_sc, kn_sc, vn_sc = refs[4 * pps:]
    del pt_ref
    j = pl.program_id(1)
    nrow = n_new * FOX_HEADS

    @pl.when(j == 0)
    def _():
        col_head = lax.broadcasted_iota(jnp.int32, (FOX_HEADS, FOX_INNER), 1) // FOX_HEAD_DIM
        own = col_head == lax.broadcasted_iota(jnp.int32, (FOX_HEADS, FOX_INNER), 0)
        q = q_ref[...].astype(F32)
        for t in range(n_new):
            qbd_sc[t * FOX_HEADS:(t + 1) * FOX_HEADS, :] = jnp.where(own, q[t:t + 1, :], 0.0).astype(BF16)
        m_sc[...] = jnp.full_like(m_sc, NEG_INF)
        l_sc[...] = jnp.zeros_like(l_sc)
        acc_sc[...] = jnp.zeros_like(acc_sc)
        carry_sc[...] = jnp.zeros_like(carry_sc)

    qbd = qbd_sc[...]

    def tile_rows(bias16):
        return jnp.concatenate([bias16] * n_new, axis=0)

    def accumulate(s_list, pv_fn_list):
        m_old = m_sc[...]
        m_new = m_old
        for s in s_list:
            m_new = jnp.maximum(m_new, jnp.max(s, axis=1, keepdims=True))
        alpha = jnp.exp2(m_old - m_new)
        l_new = alpha * l_sc[...]
        pv = jnp.zeros(acc_sc.shape, F32)
        for s, pv_fn in zip(s_list, pv_fn_list):
            p = jnp.exp2(s - m_new)
            l_new = l_new + jnp.sum(p, axis=1, keepdims=True)
            pv = pv + pv_fn(p.astype(BF16))
        m_sc[...] = m_new
        l_sc[...] = l_new
        acc_sc[...] = acc_sc[...] * alpha + pv

    carry = carry_sc[...]
    s_list, pv_fns = [], []
    for i in range(pps):
        kt = k_refs[i][...].reshape(FOX_INNER, BLOCK).astype(BF16)
        vt = v_refs[i][...].reshape(FOX_INNER, BLOCK).astype(BF16)
        s_list.append(_dot(qbd, kt) - tile_rows((carry + c_refs[i][...]) * LOG2E))
        pv_fns.append(functools.partial(lambda p, vt: _dot_nt(p, vt), vt=vt))
        carry = carry + t_refs[i][...]
    carry_sc[...] = carry
    accumulate(s_list, pv_fns)

    @pl.when(j == pl.num_programs(1) - 1)
    def _():
        rows = kn_ref.shape[0]
        kn_sc[...] = jnp.zeros_like(kn_sc)
        vn_sc[...] = jnp.zeros_like(vn_sc)
        kn_sc[0:rows, :] = kn_ref[...].astype(BF16)
        vn_sc[0:rows, :] = vn_ref[...].astype(BF16)
        bias = (carry_sc[...] + _dot_exact_r(ln_ref[...], _triu(LANES))) * LOG2E
        s = _dot_nt(qbd, kn_sc[...]) - tile_rows(bias)
        key = lax.broadcasted_iota(jnp.int32, (nrow, BLOCK), 1)
        tok = lax.broadcasted_iota(jnp.int32, (nrow, BLOCK), 0) // FOX_HEADS
        s = jnp.where(key <= tok, s, NEG_INF)
        vn = vn_sc[...]
        accumulate([s], [lambda p: _dot(p, vn)])
        o = acc_sc[...] / l_sc[...]
        row_head = lax.broadcasted_iota(jnp.int32, (nrow, FOX_INNER), 0) % FOX_HEADS
        col_head = lax.broadcasted_iota(jnp.int32, (nrow, FOX_INNER), 1) // FOX_HEAD_DIM
        o = jnp.where(row_head == col_head, o, 0.0)
        o_ref[...] = jnp.sum(o.reshape(n_new, FOX_HEADS, FOX_INNER), axis=1)


def _fox_sample(page_table, q, k_new8, v_new8, lf_new_t, cache_kt, cache_vt, c_page, tot_page, pps):
    nb, n_new = q.shape[0], q.shape[1]
    n_pages = page_table.shape[1]
    rows8 = k_new8.shape[1]
    pt_flat = page_table.reshape(-1)

    def page_idx(i):
        return lambda b, j, pt: pt[b * n_pages + j * pps + i]

    def kv_spec(i):
        f = page_idx(i)
        return pl.BlockSpec((None, None, FOX_HEADS, FOX_HEAD_DIM, BLOCK), lambda b, j, pt: (0, f(b, j, pt), 0, 0, 0))

    def c_spec(i):
        f = page_idx(i)
        return pl.BlockSpec((None, FOX_HEADS, LANES), lambda b, j, pt: (f(b, j, pt), 0, 0))

    per = lambda r, w: pl.BlockSpec((None, r, w), lambda b, j, pt: (b, 0, 0))
    in_specs = [per(n_new, FOX_INNER), per(rows8, FOX_INNER), per(rows8, FOX_INNER), per(FOX_HEADS, LANES)]
    in_specs += [kv_spec(i) for i in range(pps)] * 2 + [c_spec(i) for i in range(pps)] * 2
    nrow = n_new * FOX_HEADS
    grid_spec = pltpu.PrefetchScalarGridSpec(
        num_scalar_prefetch=1,
        grid=(nb, n_pages // pps),
        in_specs=in_specs,
        out_specs=per(n_new, FOX_INNER),
        scratch_shapes=[pltpu.VMEM((nrow, FOX_INNER), BF16), pltpu.VMEM((nrow, 1), F32), pltpu.VMEM((nrow, 1), F32),
                        pltpu.VMEM((nrow, FOX_INNER), F32), pltpu.VMEM((FOX_HEADS, LANES), F32),
                        pltpu.VMEM((BLOCK, FOX_INNER), BF16), pltpu.VMEM((BLOCK, FOX_INNER), BF16)],
    )
    return pl.pallas_call(
        functools.partial(_fox_sample_body, pps=pps, n_new=n_new),
        grid_spec=grid_spec,
        out_shape=jax.ShapeDtypeStruct((nb, n_new, FOX_INNER), F32),
        compiler_params=_params(2, 56),
        name="fox_sample",
    )(pt_flat, q, k_new8, v_new8, lf_new_t, *([cache_kt] * pps), *([cache_vt] * pps), *([c_page] * pps),
      *([tot_page] * pps))


def _ffn_weights(w_gate, w_up, w_down):
    return w_gate.astype(BF16), w_up.astype(BF16), w_down.astype(BF16)


def _pick_tile(n, candidates):
    for t in candidates:
        if n % t == 0:
            return t
    raise ValueError(f"no tile for {n}")


def kernel(x_prompt, x_sample, cache_k, cache_v, cache_logf, page_table, state_conv, state_ssm, meta_tokens, norm_ffn1, w_ffn1_gate, w_ffn1_up, w_ffn1_down, norm_mix, w_in, conv_w, conv_b, dt_bias, A_log, D_skip, ssd_norm, w_ssd_out, b_forget, w_fox_out, w_o, norm_ffn2, w_ffn2_gate, w_ffn2_up, w_ffn2_down, norm_final):
    batch, seq, _ = x_prompt.shape
    nb, n_new, _ = x_sample.shape
    depth, n_phys, page = cache_k.shape[0], cache_k.shape[1], cache_k.shape[2]
    assert depth == 1 and page == BLOCK and seq % BLOCK == 0 and n_new <= SUBLANES
    n_s = nb * n_new
    n_s_pad = -(-n_s // BLOCK) * BLOCK
    meta_blk = n_s_pad // BLOCK
    rows_p = batch * seq

    ffn1_w = _ffn_weights(w_ffn1_gate[0], w_ffn1_up[0], w_ffn1_down[0])
    ffn2_w = _ffn_weights(w_ffn2_gate[0], w_ffn2_up[0], w_ffn2_down[0])
    wz, wxbc, wdt, wq, wk, wv, wf, wgs, wgf = jnp.split(
        w_in[0], [1024, 2560, 2576, 3600, 4624, 5648, 5664, 6688], axis=1)
    w_main = jnp.concatenate([wz, wxbc, wq, wk, wv, wgs, wgf], axis=1).astype(BF16)
    w_small = jnp.pad(jnp.concatenate([wdt, wf], axis=1), ((0, 0), (0, LANES - SSD_HEADS - FOX_HEADS))).astype(BF16)
    bias_small = jnp.pad(jnp.concatenate([dt_bias[0], b_forget[0]]), (0, LANES - SSD_HEADS - FOX_HEADS))[None, :]
    row = lambda a: a.reshape(1, -1).astype(F32)
    ssd_w = (jnp.pad(conv_w[0], ((0, SUBLANES - CONV_WIDTH), (0, 0))), row(conv_b[0]),
             jnp.pad(-jnp.exp(A_log[0].astype(F32)), (0, LANES - SSD_HEADS))[None, :],
             row(jnp.repeat(D_skip[0], SSD_HEAD_DIM)), row(ssd_norm[0]), w_ssd_out[0].astype(BF16),
             *_ssd_routing_matrices())
    w_fox_b, w_o_b = w_fox_out[0].astype(BF16), w_o[0].astype(BF16)

    xp = x_prompt.reshape(rows_p, D_MODEL)
    x_small = jnp.concatenate([x_sample.reshape(n_s, D_MODEL), jnp.zeros((n_s_pad - n_s + PAD, D_MODEL), F32),
                               meta_tokens.astype(F32)], axis=0)
    rows_s = n_s_pad + BLOCK

    tm_ffn = _pick_tile(rows_p, (512, 256, 128))
    tm_in = _pick_tile(rows_p, (256, 128))

    x1 = _ffn(xp, rows_p, tm_ffn, row(norm_ffn1[0]), *ffn1_w, name="ffn1")
    tm_small = _pick_tile(rows_s, (256, 128))
    x1s = _ffn(x_small, rows_s, tm_small, row(norm_ffn1[0]), *ffn1_w, name="ffn1_small")
    z, xbc, q, k, v, gs, gf, sm = _inproj(x1, tm_in, row(norm_mix[0]), w_main, w_small, bias_small, "inproj")
    zs, xbcs, qs, ks, vs, gss, gfs, sms = _inproj(x1s, tm_small, row(norm_mix[0]), w_main, w_small, bias_small,
                                                  "inproj_small")

    o_ssd, st_p = _ssd_prompt(xbc, z, sm, xbcs, sms, meta_blk, ssd_w, batch, seq)

    tq = _pick_tile(seq, (512, 256, 128))
    c_tok, c_meta = _logf_cumsum(sm, sms, meta_blk, batch, seq, _pick_tile(seq, (512, 256, 128)))
    lo, hi = SSD_HEADS, SSD_HEADS + FOX_HEADS
    o_att = _fox_prompt(q, k, v, c_tok, ks, vs, c_meta, meta_blk, batch, seq, tq)

    y_p = _ffn(x1, rows_p, tm_ffn, row(norm_ffn2[0]), *ffn2_w, merge_args=(o_ssd, o_att, gs, gf, w_fox_b, w_o_b),
               final_norm_w=row(norm_final), name="merge_ffn2")

    pad8 = lambda a: jnp.pad(a[:n_s].reshape(nb, n_new, -1), ((0, 0), (0, SUBLANES - n_new), (0, 0)))
    hist8 = jnp.pad(state_conv[0].astype(F32), ((0, 0), (SUBLANES - (CONV_WIDTH - 1), 0), (0, 0)))
    rper = SSD_HEADS // SSD_GROUPS
    state0 = state_ssm[0].astype(F32).reshape(nb, SSD_GROUPS, rper, SSD_HEAD_DIM, SSD_STATE)
    state0 = state0.transpose(0, 1, 4, 2, 3).reshape(nb, SSD_GROUPS, SSD_STATE, GROUP_W)
    o_ssd_s8, st_s = _ssd_sample(pad8(xbcs), pad8(zs), pad8(sms), hist8, state0, ssd_w, n_new)

    lf_t = cache_logf[0].astype(F32).transpose(0, 2, 1).reshape(n_phys * FOX_HEADS, page)
    c_page, tot_page = _page_cumsum(lf_t, _pick_tile(n_phys * FOX_HEADS, (4096, 2048, 1024, 512, 256, 128, 16)))
    c_page = c_page.reshape(n_phys, FOX_HEADS, page)
    tot_page = tot_page.reshape(n_phys, FOX_HEADS, page)
    lf_new_t = jnp.pad(sms[:n_s, lo:hi].reshape(nb, n_new, FOX_HEADS).transpose(0, 2, 1),
                       ((0, 0), (0, 0), (0, LANES - n_new)))
    pps = _pick_tile(page_table.shape[1], (16, 8, 4, 2, 1))
    o_att_s = _fox_sample(page_table, qs[:n_s].reshape(nb, n_new, FOX_INNER), pad8(ks), pad8(vs), lf_new_t,
                          cache_k.transpose(0, 1, 3, 4, 2), cache_v.transpose(0, 1, 3, 4, 2), c_page, tot_page, pps)

    rows_pad = lambda a: jnp.pad(a.reshape(n_s, -1), ((0, n_s_pad - n_s), (0, 0)))
    y_s = _ffn(x1s, n_s_pad, BLOCK, row(norm_ffn2[0]), *ffn2_w,
               merge_args=(rows_pad(o_ssd_s8[:, :n_new]), rows_pad(o_att_s).astype(BF16), gss, gfs, w_fox_b, w_o_b),
               final_norm_w=row(norm_final), name="merge_ffn2_small")

    meta_rows = slice(n_s_pad + PAD, n_s_pad + BLOCK)

    def with_meta(tok, small, width):
        m = jnp.broadcast_to(small[meta_rows][None], (batch, N_META, width))
        return jnp.concatenate([m, tok.reshape(batch, seq, width)], axis=1)

    def unpack_state(st, n):
        st = st.reshape(n, SSD_GROUPS, SSD_STATE, rper, SSD_HEAD_DIM).transpose(0, 1, 3, 4, 2)
        return st.reshape(1, n, SSD_HEADS, SSD_HEAD_DIM, SSD_STATE)

    y_prompt = y_p.reshape(batch, seq, D_MODEL)
    y_sample = y_s[:n_s].reshape(nb, n_new, D_MODEL)
    k_p = with_meta(k, ks, FOX_INNER).reshape(1, batch, seq + N_META, FOX_HEADS, FOX_HEAD_DIM)
    v_p = with_meta(v, vs, FOX_INNER).reshape(1, batch, seq + N_META, FOX_HEADS, FOX_HEAD_DIM)
    lf_p = with_meta(sm[:, lo:hi], sms[:, lo:hi], FOX_HEADS)[None]
    conv_p = xbc.reshape(batch, seq, CONV_DIM)[:, seq - (CONV_WIDTH - 1):][None]
    ssm_p = unpack_state(st_p, batch)
    k_s = ks[:n_s].reshape(1, nb, n_new, FOX_HEADS, FOX_HEAD_DIM)
    v_s = vs[:n_s].reshape(1, nb, n_new, FOX_HEADS, FOX_HEAD_DIM)
    lf_s = sms[:n_s, lo:hi].reshape(1, nb, n_new, FOX_HEADS)
    conv_s = jnp.concatenate([state_conv[0].astype(F32), xbcs[:n_s].reshape(nb, n_new, CONV_DIM)], axis=1)[:, n_new:][None]
    ssm_s = unpack_state(st_s, nb)
    return (y_prompt, y_sample, k_p, v_p, lf_p, conv_p, ssm_p, k_s, v_s, lf_s, conv_s, ssm_s)
```

```python
import functools

import jax
import jax.numpy as jnp
from jax import lax
from jax.experimental import pallas as pl
from jax.experimental.pallas import tpu as pltpu

F32 = jnp.float32
BF16 = jnp.bfloat16

D_MODEL = 1024
D_FF = 2816
NORM_EPS = 1e-6
N_META = 16
BLOCK = 128
PAD = BLOCK - N_META
SSD_HEADS = 16
SSD_HEAD_DIM = 64
SSD_INNER = SSD_HEADS * SSD_HEAD_DIM
SSD_STATE = 128
SSD_GROUPS = 2
GROUP_W = SSD_INNER // SSD_GROUPS
CONV_WIDTH = 4
CONV_DIM = SSD_INNER + 2 * SSD_GROUPS * SSD_STATE
FOX_HEADS = 16
FOX_HEAD_DIM = 64
FOX_INNER = FOX_HEADS * FOX_HEAD_DIM
FOX_SCALE = FOX_HEAD_DIM ** -0.5
NEG_INF = -1e30

LANES = 128
SUBLANES = 8
V7X_VMEM_BYTES = 64 * 1024 * 1024

FF_CHUNK = D_FF // 2
LOG2E = 1.4426950408889634
Q_SCALE = FOX_SCALE * LOG2E
HIST = SUBLANES


def _dot(a, b):
    return jnp.dot(a, b, preferred_element_type=F32)


def _dot_nt(a, b):
    return lax.dot_general(a, b, (((1,), (1,)), ((), ())), preferred_element_type=F32)


def _split3(x):
    hi = x.astype(BF16)
    r = x - hi.astype(F32)
    mid = r.astype(BF16)
    lo = (r - mid.astype(F32)).astype(BF16)
    return hi, mid, lo


def _dot_exact(a01, x):
    hi, mid, lo = _split3(x)
    return _dot(a01, hi) + _dot(a01, mid) + _dot(a01, lo)


def _dot_exact_r(x, b01):
    hi, mid, lo = _split3(x)
    return _dot(hi, b01) + _dot(mid, b01) + _dot(lo, b01)


def _transpose_exact(x, eye):
    hi, mid, lo = _split3(x)
    return _dot_nt(eye, hi) + _dot_nt(eye, mid) + _dot_nt(eye, lo)


def _tril(n):
    return (lax.broadcasted_iota(jnp.int32, (n, n), 1) <= lax.broadcasted_iota(jnp.int32, (n, n), 0)).astype(BF16)


def _triu(n):
    return (lax.broadcasted_iota(jnp.int32, (n, n), 0) <= lax.broadcasted_iota(jnp.int32, (n, n), 1)).astype(BF16)


def _rms(x):
    return x * lax.rsqrt(jnp.mean(x * x, axis=-1, keepdims=True) + NORM_EPS)


def _silu(x):
    return x * jax.nn.sigmoid(x)


def _params(n_axes, vmem_mib):
    return pltpu.CompilerParams(dimension_semantics=("arbitrary",) * n_axes,
                                vmem_limit_bytes=min(vmem_mib * 1024 * 1024, V7X_VMEM_BYTES - 4 * 1024 * 1024))


def _resident(shape):
    nd = len(shape)
    return pl.BlockSpec(shape, lambda *_: (0,) * nd, pipeline_mode=pl.Buffered(1))


def _ffn_body(*refs, merge, final_norm):
    refs = list(refs)
    x_ref = refs.pop(0)
    if merge:
        os_ref, oa_ref, gs_ref, gf_ref, wf_ref, wo_ref = refs[:6]
        refs = refs[6:]
    nw_ref, wg_ref, wu_ref, wd_ref = refs[:4]
    refs = refs[4:]
    if final_norm:
        fnw_ref = refs.pop(0)
    o_ref, hn_ref = refs

    x = x_ref[...]
    if merge:
        o_fox = _dot(oa_ref[...], wf_ref[...])
        mix = jax.nn.sigmoid(gs_ref[...]) * os_ref[...] + jax.nn.sigmoid(gf_ref[...]) * o_fox
        x = x + _dot(mix.astype(BF16), wo_ref[...])
    hn_ref[...] = (_rms(x) * nw_ref[...]).astype(BF16)
    y = x
    for c in range(D_FF // FF_CHUNK):
        cols = slice(c * FF_CHUNK, (c + 1) * FF_CHUNK)
        g = _dot(hn_ref[...], wg_ref[:, cols])
        u = _dot(hn_ref[...], wu_ref[:, cols])
        y = y + 0.5 * _dot((_silu(g) * u).astype(BF16), wd_ref[cols, :])
    if final_norm:
        y = _rms(y) * fnw_ref[...]
    o_ref[...] = y


def _ffn(x, n_rows, tm, norm_w, wg, wu, wd, merge_args=None, final_norm_w=None, name="ffn"):
    merge = merge_args is not None
    row_spec = pl.BlockSpec((tm, D_MODEL), lambda i: (i, 0))
    args, specs = [x], [row_spec]
    if merge:
        o_ssd, o_att, g_ssd, g_fox, w_fox, w_o = merge_args
        args += [o_ssd, o_att, g_ssd, g_fox, w_fox, w_o]
        specs += [row_spec] * 4 + [_resident(w_fox.shape), _resident(w_o.shape)]
    args += [norm_w, wg, wu, wd]
    specs += [_resident(norm_w.shape), _resident(wg.shape), _resident(wu.shape), _resident(wd.shape)]
    if final_norm_w is not None:
        args.append(final_norm_w)
        specs.append(_resident(final_norm_w.shape))
    return pl.pallas_call(
        functools.partial(_ffn_body, merge=merge, final_norm=final_norm_w is not None),
        grid=(n_rows // tm,),
        in_specs=specs,
        out_specs=row_spec,
        out_shape=jax.ShapeDtypeStruct((n_rows, D_MODEL), F32),
        scratch_shapes=[pltpu.VMEM((tm, D_MODEL), BF16)],
        compiler_params=_params(1, 60),
        name=name,
    )(*args)


_MAIN_WIDTHS = (SSD_INNER, CONV_DIM, FOX_INNER, D_MODEL, D_MODEL)
_Q_INDEX = 2


def _inproj_body(x_ref, nw_ref, wm_ref, wkv_ref, ws_ref, bias_ref, *refs):
    n_main = len(_MAIN_WIDTHS)
    outs, (kt_ref, vt_ref, sm_ref, hn_ref) = refs[:n_main], refs[n_main:]
    hn_ref[...] = (_rms(x_ref[...]) * nw_ref[...]).astype(BF16)
    off = 0
    for i, (ref, w) in enumerate(zip(outs, _MAIN_WIDTHS)):
        val = _dot(hn_ref[...], wm_ref[:, off:off + w])
        ref[...] = (val * Q_SCALE).astype(BF16) if i == _Q_INDEX else val
        off += w
    kt_ref[...] = _dot_nt(wkv_ref[0:FOX_INNER, :], hn_ref[...])
    vt_ref[...] = _dot_nt(wkv_ref[FOX_INNER:, :], hn_ref[...])
    raw = _dot(hn_ref[...], ws_ref[...]) + bias_ref[...]
    t = jnp.log1p(jnp.exp(-jnp.abs(raw)))
    lane = lax.broadcasted_iota(jnp.int32, raw.shape, 1)
    sm_ref[...] = jnp.where(lane < SSD_HEADS, jnp.maximum(raw, 0.0) + t,
                            jnp.where(lane < SSD_HEADS + FOX_HEADS, jnp.minimum(raw, 0.0) - t, 0.0))


def _inproj(x, tm, seq_rows, norm_w, w_main, w_kv_t, w_small, bias_small, name):
    rows = x.shape[0]
    nseq, per_seq = rows // seq_rows, seq_rows // tm
    row = lambda w: pl.BlockSpec((tm, w), lambda i: (i, 0))
    row_shape = lambda i, w: jax.ShapeDtypeStruct((rows, w), BF16 if i == _Q_INDEX else F32)
    t_spec = pl.BlockSpec((None, FOX_INNER, tm), lambda i: (i // per_seq, 0, i % per_seq))
    t_shape = jax.ShapeDtypeStruct((nseq, FOX_INNER, seq_rows), F32)
    return pl.pallas_call(
        _inproj_body,
        grid=(rows // tm,),
        in_specs=[row(D_MODEL), _resident(norm_w.shape), _resident(w_main.shape), _resident(w_kv_t.shape),
                  _resident(w_small.shape), _resident(bias_small.shape)],
        out_specs=[row(w) for w in _MAIN_WIDTHS] + [t_spec, t_spec, row(LANES)],
        out_shape=[row_shape(i, w) for i, w in enumerate(_MAIN_WIDTHS)] + [t_shape, t_shape, row_shape(-1, LANES)],
        scratch_shapes=[pltpu.VMEM((tm, D_MODEL), BF16)],
        compiler_params=_params(1, 60),
        name=name,
    )(x, norm_w, w_main, w_kv_t, w_small, bias_small)


def _cumsum_rows(x):
    return _dot_exact(_tril(x.shape[0]), x)


def _logf_cumsum_body(lt_ref, lm_ref, ct_ref, cm_ref, carry_ref):
    @pl.when(pl.program_id(1) == 0)
    def _():
        row = lax.broadcasted_iota(jnp.int32, (BLOCK, LANES), 0)
        cm = _cumsum_rows(jnp.where(row >= PAD, lm_ref[...], 0.0))
        cm_ref[...] = cm
        carry_ref[...] = cm[BLOCK - 1:BLOCK, :]

    c = _cumsum_rows(lt_ref[...]) + carry_ref[...]
    ct_ref[...] = c
    carry_ref[...] = c[c.shape[0] - 1:, :]


def _logf_cumsum(sm_tok, sm_small, meta_blk, batch, seq, tb):
    nt = seq // tb
    return pl.pallas_call(
        _logf_cumsum_body,
        grid=(batch, nt),
        in_specs=[pl.BlockSpec((tb, LANES), lambda b, j: (b * nt + j, 0)),
                  pl.BlockSpec((BLOCK, LANES), lambda b, j: (meta_blk, 0))],
        out_specs=[pl.BlockSpec((tb, LANES), lambda b, j: (b * nt + j, 0)),
                   pl.BlockSpec((None, BLOCK, LANES), lambda b, j: (b, 0, 0))],
        out_shape=[jax.ShapeDtypeStruct((batch * seq, LANES), F32),
                   jax.ShapeDtypeStruct((batch, BLOCK, LANES), F32)],
        scratch_shapes=[pltpu.VMEM((1, LANES), F32)],
        compiler_params=_params(2, 32),
        name="logf_cumsum",
    )(sm_tok, sm_small)


_BIAS_PIECES = 3
_Q_GROUP = 512


def _fox_prompt_body(q_ref, kt_ref, vt_ref, ct_ref, km_ref, vm_ref, cm_ref, o_ref,
                     k0_sc, k1_sc, v0_sc, v1_sc, km0_sc, km1_sc, vm0_sc, vm1_sc, sa_sc, sb_sc, *, tq):
    hp = pl.program_id(1)
    qi = pl.program_id(2)
    lane = lax.broadcasted_iota(jnp.int32, (1, LANES), 1)
    is0 = lane < FOX_HEAD_DIM
    extra0, extra1 = FOX_HEAD_DIM, 0

    @pl.when(qi == 0)
    def _():
        src = lax.broadcasted_iota(jnp.int32, (_BIAS_PIECES * LANES, LANES), 0)
        dst = lax.broadcasted_iota(jnp.int32, (_BIAS_PIECES * LANES, LANES), 1)
        c_lane = SSD_HEADS + 2 * hp

        def route(head, base):
            return jnp.where((src % LANES == c_lane + head) & (dst == base + src // LANES), 1.0, 0.0).astype(BF16)

        route0, route1 = route(0, extra0), route(1, extra1)
        sub = lax.broadcasted_iota(jnp.int32, (LANES, 1), 0)
        top_rows = sub < FOX_HEAD_DIM
        one_row0 = jnp.where(sub == extra0, 1.0, 0.0)
        one_row1 = jnp.where(sub == extra1, 1.0, 0.0)

        def eye_of(n):
            return (lax.broadcasted_iota(jnp.int32, (n, n), 0) == lax.broadcasted_iota(jnp.int32, (n, n), 1)).astype(BF16)

        def prep(kt, vt, bias, eye):
            k = _dot_nt(eye, kt.astype(BF16))
            pieces = jnp.concatenate(_split3(bias), axis=1)
            k0 = (jnp.where(is0, k, 0.0) + _dot(pieces, route0)).astype(BF16)
            k1 = (jnp.where(is0, 0.0, k) + _dot(pieces, route1)).astype(BF16)
            v0 = jnp.where(top_rows, vt, one_row0).astype(BF16)
            v1 = jnp.where(top_rows, one_row1, vt).astype(BF16)
            return k0, k1, v0, v1

        row = lax.broadcasted_iota(jnp.int32, (BLOCK, LANES), 0)
        bias_m = jnp.where(row >= PAD, cm_ref[...] * (-LOG2E), NEG_INF)
        km0_sc[...], km1_sc[...], vm0_sc[...], vm1_sc[...] = prep(km_ref[...], vm_ref[...], bias_m, eye_of(BLOCK))
        eye_q = eye_of(tq)
        for i in range(k0_sc.shape[0]):
            blk = slice(i * tq, (i + 1) * tq)
            k0_sc[i], k1_sc[i], v0_sc[i], v1_sc[i] = prep(kt_ref[:, blk], vt_ref[:, blk],
                                                         ct_ref[blk, :] * (-LOG2E), eye_q)

    q = q_ref[...].astype(F32)
    ones0 = jnp.where((lane >= extra0) & (lane < extra0 + _BIAS_PIECES), 1.0, 0.0)
    ones1 = jnp.where((lane >= extra1) & (lane < extra1 + _BIAS_PIECES), 1.0, 0.0)
    q0 = jnp.where(is0, q, ones0).astype(BF16)
    q1 = jnp.where(is0, ones1, q).astype(BF16)

    def consume(m, acc, s, vt, mask):
        if mask is not None:
            s = jnp.where(mask, s, NEG_INF)
        m_new = jnp.maximum(m, jnp.max(s, axis=0, keepdims=True))
        p = jnp.exp2(s - m_new).astype(BF16)
        return m_new, acc * jnp.exp2(m - m_new) + _dot(vt, p)

    neg = jnp.full((1, tq), NEG_INF, F32)
    zero = jnp.zeros((LANES, tq), F32)
    m0, acc0 = consume(neg, zero, _dot_nt(km0_sc[...], q0), vm0_sc[...], None)
    m1, acc1 = consume(neg, zero, _dot_nt(km1_sc[...], q1), vm1_sc[...], None)

    def issue(j, s_ref):
        s_ref[0] = _dot_nt(k0_sc[j], q0)
        s_ref[1] = _dot_nt(k1_sc[j], q1)

    def retire(j, s_ref, carry, mask=None):
        m0, acc0, m1, acc1 = carry
        m0, acc0 = consume(m0, acc0, s_ref[0], v0_sc[j], mask)
        m1, acc1 = consume(m1, acc1, s_ref[1], v1_sc[j], mask)
        return m0, acc0, m1, acc1

    def pair(i, carry):
        issue(2 * i + 1, sb_sc)
        carry = retire(2 * i, sa_sc, carry)
        issue(2 * i + 2, sa_sc)
        return retire(2 * i + 1, sb_sc, carry)

    issue(0, sa_sc)
    carry = lax.fori_loop(0, qi // 2, pair, (m0, acc0, m1, acc1))
    causal = lax.broadcasted_iota(jnp.int32, (tq, 1), 0) <= lax.broadcasted_iota(jnp.int32, (1, tq), 1)
    top = lax.broadcasted_iota(jnp.int32, (LANES, 1), 0) < FOX_HEAD_DIM

    def finish(carry):
        _, acc0, _, acc1 = carry
        o_t = jnp.where(top, acc0 / acc0[extra0:extra0 + 1, :], acc1 / acc1[extra1:extra1 + 1, :])
        o_ref[...] = o_t.T.astype(BF16)

    @pl.when(qi % 2 == 0)
    def _():
        finish(retire(qi, sa_sc, carry, causal))

    @pl.when(qi % 2 == 1)
    def _():
        issue(qi, sb_sc)
        finish(retire(qi, sb_sc, retire(qi - 1, sa_sc, carry), causal))


def _fox_prompt(q, k_t, v_t, c_tok, k_small_t, v_small_t, c_meta, meta_blk, batch, seq, tq):
    nq = seq // tq
    npair = FOX_INNER // LANES
    kv_spec = pl.BlockSpec((None, LANES, seq), lambda b, hp, i: (b, hp, 0))
    meta_spec = pl.BlockSpec((None, LANES, BLOCK), lambda b, hp, i: (0, hp, meta_blk))
    blk_spec = pl.BlockSpec((tq, LANES), lambda b, hp, i: (b * nq + i, hp))
    return pl.pallas_call(
        functools.partial(_fox_prompt_body, tq=tq),
        grid=(batch, npair, nq),
        in_specs=[blk_spec, kv_spec, kv_spec, pl.BlockSpec((seq, LANES), lambda b, hp, i: (b, 0)),
                  meta_spec, meta_spec, pl.BlockSpec((None, BLOCK, LANES), lambda b, hp, i: (b, 0, 0))],
        out_specs=blk_spec,
        out_shape=jax.ShapeDtypeStruct((batch * seq, FOX_INNER), BF16),
        scratch_shapes=[pltpu.VMEM((nq, tq, LANES), BF16)] * 2 + [pltpu.VMEM((nq, LANES, tq), BF16)] * 2
                       + [pltpu.VMEM((BLOCK, LANES), BF16)] * 4 + [pltpu.VMEM((2, tq, tq), F32)] * 2,
        compiler_params=_params(3, 48),
        name="fox_prompt",
    )(q, k_t, v_t, c_tok, k_small_t, v_small_t, c_meta)


def _ssd_chunk(xbuf_ref, dt, z, st_ref, ybuf_ref, w_refs):
    L = BLOCK
    cw_ref, cb_ref, a_ref, dskip_ref, nw_ref, wout_ref, tril_ref, eye_ref, e_wide_ref, e_head_ref = w_refs
    eye = eye_ref[...]
    conv = cb_ref[...]
    for w in range(CONV_WIDTH):
        conv = conv + xbuf_ref[HIST - (CONV_WIDTH - 1) + w:HIST - (CONV_WIDTH - 1) + w + L, :] * cw_ref[w:w + 1, :]
    xc = _silu(conv)
    xs = xc[:, :SSD_INNER]
    gn = SSD_GROUPS * SSD_STATE

    a_col = _dot_exact(tril_ref[...], dt * a_ref[...])
    a_row = _transpose_exact(a_col, eye)
    dt_row = _transpose_exact(dt, eye)
    a_bc = _dot_exact_r(a_col, e_wide_ref[...])
    e_head = e_head_ref[...]
    a_exp = _dot_exact_r(a_col, e_head)
    decay_in = jnp.exp(a_exp)
    a_last = a_exp[L - 1:L, :]
    to_end = jnp.exp(a_last - a_exp) * _dot_exact_r(dt, e_head)
    xw = (xs * to_end).astype(BF16)
    chunk_decay = decay_in[L - 1:L, :]

    causal = lax.broadcasted_iota(jnp.int32, (L, L), 1) <= lax.broadcasted_iota(jnp.int32, (L, L), 0)
    is0 = lax.broadcasted_iota(jnp.int32, (1, LANES), 1) < SSD_HEAD_DIM
    heads_per_group = SSD_HEADS // SSD_GROUPS
    for g in range(SSD_GROUPS):
        b_g = xc[:, SSD_INNER + g * SSD_STATE:SSD_INNER + (g + 1) * SSD_STATE].astype(BF16)
        c_g = xc[:, SSD_INNER + gn + g * SSD_STATE:SSD_INNER + gn + (g + 1) * SSD_STATE].astype(BF16)
        cb = _dot_nt(c_g, b_g)
        lo, hi = g * GROUP_W, (g + 1) * GROUP_W
        h_in = st_ref[g]
        y_off = _dot(c_g, h_in.astype(BF16)) * decay_in[:, lo:hi]
        ybuf_ref[:, lo:hi] = y_off
        b_t = _dot_nt(eye, b_g).astype(BF16)
        st_ref[g] = h_in * chunk_decay[:, lo:hi] + _dot(b_t, xw[:, lo:hi])
        for pair in range(heads_per_group // 2):
            h0 = g * heads_per_group + 2 * pair
            c0 = h0 * SSD_HEAD_DIM
            x_pair = xs[:, c0:c0 + LANES]
            y_pair = jnp.zeros((L, LANES), F32)
            for k, xm in ((0, jnp.where(is0, x_pair, 0.0)), (1, jnp.where(is0, 0.0, x_pair))):
                h = h0 + k
                seg = a_bc[:, h * LANES:(h + 1) * LANES] - a_row[h:h + 1, :]
                wgt = cb * jnp.exp(jnp.where(causal, seg, -jnp.inf)) * dt_row[h:h + 1, :]
                y_pair = y_pair + _dot(wgt.astype(BF16), xm.astype(BF16))
            ybuf_ref[:, c0:c0 + LANES] += y_pair

    y = (ybuf_ref[...] + xs * dskip_ref[...]) * _silu(z)
    yn = (_rms(y) * nw_ref[...]).astype(BF16)
    return _dot(yn, wout_ref[...])


_SSD_WEIGHT_SHAPES = ((SUBLANES, CONV_DIM), (1, CONV_DIM), (1, LANES), (1, SSD_INNER), (1, SSD_INNER),
                      (SSD_INNER, D_MODEL),
                      (BLOCK, BLOCK), (BLOCK, BLOCK), (LANES, SSD_HEADS * LANES), (LANES, SSD_INNER))


def _ssd_routing_matrices():
    r = jnp.arange(LANES)[:, None]
    tril = (jnp.arange(BLOCK)[None, :] <= jnp.arange(BLOCK)[:, None]).astype(BF16)
    eye = jnp.eye(BLOCK, dtype=BF16)
    e_wide = (jnp.arange(SSD_HEADS * LANES)[None, :] // LANES == r).astype(BF16)
    e_head = (jnp.arange(SSD_INNER)[None, :] // SSD_HEAD_DIM == r).astype(BF16)
    return tril, eye, e_wide, e_head


def _ssd_prompt_body(xt_ref, zt_ref, st_ref_in, xm_ref, sm_ref, *rest):
    w_refs = rest[:len(_SSD_WEIGHT_SHAPES)]
    o_ref, state_ref, xbuf_ref, st_ref, ybuf_ref = rest[len(_SSD_WEIGHT_SHAPES):]
    c = pl.program_id(1)
    lane = lax.broadcasted_iota(jnp.int32, (BLOCK, LANES), 1)
    row = lax.broadcasted_iota(jnp.int32, (BLOCK, LANES), 0)

    @pl.when(c == 0)
    def _():
        st_ref[...] = jnp.zeros_like(st_ref)
        xbuf_ref[0:HIST, :] = jnp.zeros((HIST, CONV_DIM), F32)
        xbuf_ref[HIST:, :] = xm_ref[...]

    @pl.when(c > 0)
    def _():
        xbuf_ref[HIST:, :] = xt_ref[...]

    dt = jnp.where(c == 0,
                   jnp.where((lane < SSD_HEADS) & (row >= PAD), sm_ref[...], 0.0),
                   jnp.where(lane < SSD_HEADS, st_ref_in[...], 0.0))
    o_ref[...] = _ssd_chunk(xbuf_ref, dt, zt_ref[...], st_ref, ybuf_ref, w_refs)
    xbuf_ref[0:HIST, :] = xbuf_ref[BLOCK:BLOCK + HIST, :]

    @pl.when(c == pl.num_programs(1) - 1)
    def _():
        state_ref[...] = st_ref[...]


def _ssd_prompt(xbc, z, sm, xbc_small, sm_small, meta_blk, weights, batch, seq):
    nc = seq // BLOCK
    tok = lambda w: pl.BlockSpec((BLOCK, w), lambda b, c: (b * nc + jnp.maximum(c - 1, 0), 0))
    meta = lambda w: pl.BlockSpec((BLOCK, w), lambda b, c: (meta_blk, 0))
    return pl.pallas_call(
        _ssd_prompt_body,
        grid=(batch, nc + 1),
        in_specs=[tok(CONV_DIM), tok(SSD_INNER), tok(LANES), meta(CONV_DIM), meta(LANES)]
                 + [_resident(s) for s in _SSD_WEIGHT_SHAPES],
        out_specs=[tok(D_MODEL),
                   pl.BlockSpec((None, SSD_GROUPS, SSD_STATE, GROUP_W), lambda b, c: (b, 0, 0, 0))],
        out_shape=[jax.ShapeDtypeStruct((batch * seq, D_MODEL), F32),
                   jax.ShapeDtypeStruct((batch, SSD_GROUPS, SSD_STATE, GROUP_W), F32)],
        scratch_shapes=[pltpu.VMEM((HIST + BLOCK, CONV_DIM), F32),
                        pltpu.VMEM((SSD_GROUPS, SSD_STATE, GROUP_W), F32),
                        pltpu.VMEM((BLOCK, SSD_INNER), F32)],
        compiler_params=_params(2, 40),
        name="ssd_prompt",
    )(xbc, z, sm, xbc_small, sm_small, *weights)


def _ssd_sample_body(x_ref, z_ref, sm_ref, hist_ref, state0_ref, *rest, n_new):
    w_refs = rest[:len(_SSD_WEIGHT_SHAPES)]
    o_ref, state_ref, xbuf_ref, st_ref, ybuf_ref, dt_ref, zbuf_ref = rest[len(_SSD_WEIGHT_SHAPES):]
    rows = x_ref.shape[0]
    xbuf_ref[0:HIST, :] = hist_ref[...]
    xbuf_ref[HIST:HIST + rows, :] = x_ref[...]
    xbuf_ref[HIST + rows:, :] = jnp.zeros((BLOCK - rows, CONV_DIM), F32)
    lane = lax.broadcasted_iota(jnp.int32, (rows, LANES), 1)
    row = lax.broadcasted_iota(jnp.int32, (rows, LANES), 0)
    dt_ref[...] = jnp.zeros_like(dt_ref)
    dt_ref[0:rows, :] = jnp.where((lane < SSD_HEADS) & (row < n_new), sm_ref[...], 0.0)
    zbuf_ref[...] = jnp.zeros_like(zbuf_ref)
    zbuf_ref[0:rows, :] = z_ref[...]
    st_ref[...] = state0_ref[...]
    out = _ssd_chunk(xbuf_ref, dt_ref[...], zbuf_ref[...], st_ref, ybuf_ref, w_refs)
    o_ref[...] = out[0:rows, :]
    state_ref[...] = st_ref[...]


def _ssd_sample(x8, z8, sm8, hist8, state0, weights, n_new):
    nb, rows = x8.shape[0], x8.shape[1]
    per = lambda w: pl.BlockSpec((None, rows, w), lambda b: (b, 0, 0))
    st_spec = pl.BlockSpec((None, SSD_GROUPS, SSD_STATE, GROUP_W), lambda b: (b, 0, 0, 0))
    return pl.pallas_call(
        functools.partial(_ssd_sample_body, n_new=n_new),
        grid=(nb,),
        in_specs=[per(CONV_DIM), per(SSD_INNER), per(LANES), per(CONV_DIM), st_spec]
                 + [_resident(s) for s in _SSD_WEIGHT_SHAPES],
        out_specs=[per(D_MODEL), st_spec],
        out_shape=[jax.ShapeDtypeStruct((nb, rows, D_MODEL), F32),
                   jax.ShapeDtypeStruct((nb, SSD_GROUPS, SSD_STATE, GROUP_W), F32)],
        scratch_shapes=[pltpu.VMEM((HIST + BLOCK, CONV_DIM), F32),
                        pltpu.VMEM((SSD_GROUPS, SSD_STATE, GROUP_W), F32),
                        pltpu.VMEM((BLOCK, SSD_INNER), F32),
                        pltpu.VMEM((BLOCK, LANES), F32),
                        pltpu.VMEM((BLOCK, SSD_INNER), F32)],
        compiler_params=_params(1, 40),
        name="ssd_sample",
    )(x8, z8, sm8, hist8, state0, *weights)


def _page_cumsum_body(x_ref, c_ref, tot_ref):
    x = x_ref[...]
    c_ref[...] = _dot_exact_r(x, _triu(LANES))
    tot_ref[...] = _dot_exact_r(x, jnp.ones((LANES, LANES), BF16))


def _page_cumsum(lf_t, tr):
    rows = lf_t.shape[0]
    spec = pl.BlockSpec((tr, LANES), lambda i: (i, 0))
    return pl.pallas_call(
        _page_cumsum_body,
        grid=(rows // tr,),
        in_specs=[spec],
        out_specs=[spec, spec],
        out_shape=[jax.ShapeDtypeStruct((rows, LANES), F32)] * 2,
        compiler_params=_params(1, 32),
        name="page_cumsum",
    )(lf_t)


def _fox_sample_body(pt_ref, q_ref, kn_ref, vn_ref, ln_ref, *refs, pps, n_new):
    k_refs, v_refs, c_refs, t_refs = refs[:pps], refs[pps:2 * pps], refs[2 * pps:3 * pps], refs[3 * pps:4 * pps]
    o_ref, qbd_sc, m_sc, l_sc, acc_sc, carry_sc, kn_sc, vn_sc = refs[4 * pps:]
    del pt_ref
    j = pl.program_id(1)
    nrow = n_new * FOX_HEADS

    @pl.when(j == 0)
    def _():
        col_head = lax.broadcasted_iota(jnp.int32, (FOX_HEADS, FOX_INNER), 1) // FOX_HEAD_DIM
        own = col_head == lax.broadcasted_iota(jnp.int32, (FOX_HEADS, FOX_INNER), 0)
        q = q_ref[...].astype(F32)
        for t in range(n_new):
            qbd_sc[t * FOX_HEADS:(t + 1) * FOX_HEADS, :] = jnp.where(own, q[t:t + 1, :], 0.0).astype(BF16)
        m_sc[...] = jnp.full_like(m_sc, NEG_INF)
        l_sc[...] = jnp.zeros_like(l_sc)
        acc_sc[...] = jnp.zeros_like(acc_sc)
        carry_sc[...] = jnp.zeros_like(carry_sc)

    qbd = qbd_sc[...]

    def tile_rows(bias16):
        return jnp.concatenate([bias16] * n_new, axis=0)

    def accumulate(s_list, pv_fn_list):
        m_old = m_sc[...]
        m_new = m_old
        for s in s_list:
            m_new = jnp.maximum(m_new, jnp.max(s, axis=1, keepdims=True))
        alpha = jnp.exp2(m_old - m_new)
        l_new = alpha * l_sc[...]
        pv = jnp.zeros(acc_sc.shape, F32)
        for s, pv_fn in zip(s_list, pv_fn_list):
            p = jnp.exp2(s - m_new)
            l_new = l_new + jnp.sum(p, axis=1, keepdims=True)
            pv = pv + pv_fn(p.astype(BF16))
        m_sc[...] = m_new
        l_sc[...] = l_new
        acc_sc[...] = acc_sc[...] * alpha + pv

    carry = carry_sc[...]
    s_list, pv_fns = [], []
    for i in range(pps):
        kt = k_refs[i][...].reshape(FOX_INNER, BLOCK).astype(BF16)
        vt = v_refs[i][...].reshape(FOX_INNER, BLOCK).astype(BF16)
        s_list.append(_dot(qbd, kt) - tile_rows((carry + c_refs[i][...]) * LOG2E))
        pv_fns.append(functools.partial(lambda p, vt: _dot_nt(p, vt), vt=vt))
        carry = carry + t_refs[i][...]
    carry_sc[...] = carry
    accumulate(s_list, pv_fns)

    @pl.when(j == pl.num_programs(1) - 1)
    def _():
        rows = kn_ref.shape[0]
        kn_sc[...] = jnp.zeros_like(kn_sc)
        vn_sc[...] = jnp.zeros_like(vn_sc)
        kn_sc[0:rows, :] = kn_ref[...].astype(BF16)
        vn_sc[0:rows, :] = vn_ref[...].astype(BF16)
        bias = (carry_sc[...] + _dot_exact_r(ln_ref[...], _triu(LANES))) * LOG2E
        s = _dot_nt(qbd, kn_sc[...]) - tile_rows(bias)
        key = lax.broadcasted_iota(jnp.int32, (nrow, BLOCK), 1)
        tok = lax.broadcasted_iota(jnp.int32, (nrow, BLOCK), 0) // FOX_HEADS
        s = jnp.where(key <= tok, s, NEG_INF)
        vn = vn_sc[...]
        accumulate([s], [lambda p: _dot(p, vn)])
        o = acc_sc[...] / l_sc[...]
        row_head = lax.broadcasted_iota(jnp.int32, (nrow, FOX_INNER), 0) % FOX_HEADS
        col_head = lax.broadcasted_iota(jnp.int32, (nrow, FOX_INNER), 1) // FOX_HEAD_DIM
        o = jnp.where(row_head == col_head, o, 0.0)
        o_ref[...] = jnp.sum(o.reshape(n_new, FOX_HEADS, FOX_INNER), axis=1)


def _fox_sample(page_table, q, k_new8, v_new8, lf_new_t, cache_kt, cache_vt, c_page, tot_page, pps):
    nb, n_new = q.shape[0], q.shape[1]
    n_pages = page_table.shape[1]
    rows8 = k_new8.shape[1]
    pt_flat = page_table.reshape(-1)

    def page_idx(i):
        return lambda b, j, pt: pt[b * n_pages + j * pps + i]

    def kv_spec(i):
        f = page_idx(i)
        return pl.BlockSpec((None, None, FOX_HEADS, FOX_HEAD_DIM, BLOCK), lambda b, j, pt: (0, f(b, j, pt), 0, 0, 0))

    def c_spec(i):
        f = page_idx(i)
        return pl.BlockSpec((None, FOX_HEADS, LANES), lambda b, j, pt: (f(b, j, pt), 0, 0))

    per = lambda r, w: pl.BlockSpec((None, r, w), lambda b, j, pt: (b, 0, 0))
    in_specs = [per(n_new, FOX_INNER), per(rows8, FOX_INNER), per(rows8, FOX_INNER), per(FOX_HEADS, LANES)]
    in_specs += [kv_spec(i) for i in range(pps)] * 2 + [c_spec(i) for i in range(pps)] * 2
    nrow = n_new * FOX_HEADS
    grid_spec = pltpu.PrefetchScalarGridSpec(
        num_scalar_prefetch=1,
        grid=(nb, n_pages // pps),
        in_specs=in_specs,
        out_specs=per(n_new, FOX_INNER),
        scratch_shapes=[pltpu.VMEM((nrow, FOX_INNER), BF16), pltpu.VMEM((nrow, 1), F32), pltpu.VMEM((nrow, 1), F32),
                        pltpu.VMEM((nrow, FOX_INNER), F32), pltpu.VMEM((FOX_HEADS, LANES), F32),
                        pltpu.VMEM((BLOCK, FOX_INNER), BF16), pltpu.VMEM((BLOCK, FOX_INNER), BF16)],
    )
    return pl.pallas_call(
        functools.partial(_fox_sample_body, pps=pps, n_new=n_new),
        grid_spec=grid_spec,
        out_shape=jax.ShapeDtypeStruct((nb, n_new, FOX_INNER), F32),
        compiler_params=_params(2, 56),
        name="fox_sample",
    )(pt_flat, q, k_new8, v_new8, lf_new_t, *([cache_kt] * pps), *([cache_vt] * pps), *([c_page] * pps),
      *([tot_page] * pps))


def _ffn_weights(w_gate, w_up, w_down):
    return w_gate.astype(BF16), w_up.astype(BF16), w_down.astype(BF16)


def _pick_tile(n, candidates):
    for t in candidates:
        if n % t == 0:
            return t
    raise ValueError(f"no tile for {n}")


def kernel(x_prompt, x_sample, cache_k, cache_v, cache_logf, page_table, state_conv, state_ssm, meta_tokens, norm_ffn1, w_ffn1_gate, w_ffn1_up, w_ffn1_down, norm_mix, w_in, conv_w, conv_b, dt_bias, A_log, D_skip, ssd_norm, w_ssd_out, b_forget, w_fox_out, w_o, norm_ffn2, w_ffn2_gate, w_ffn2_up, w_ffn2_down, norm_final):
    batch, seq, _ = x_prompt.shape
    nb, n_new, _ = x_sample.shape
    depth, n_phys, page = cache_k.shape[0], cache_k.shape[1], cache_k.shape[2]
    assert depth == 1 and page == BLOCK and seq % BLOCK == 0 and n_new <= SUBLANES
    n_s = nb * n_new
    n_s_pad = -(-n_s // BLOCK) * BLOCK
    meta_blk = n_s_pad // BLOCK
    rows_p = batch * seq

    ffn1_w = _ffn_weights(w_ffn1_gate[0], w_ffn1_up[0], w_ffn1_down[0])
    ffn2_w = _ffn_weights(w_ffn2_gate[0], w_ffn2_up[0], w_ffn2_down[0])
    wz, wxbc, wdt, wq, wk, wv, wf, wgs, wgf = jnp.split(
        w_in[0], [1024, 2560, 2576, 3600, 4624, 5648, 5664, 6688], axis=1)
    w_main = jnp.concatenate([wz, wxbc, wq, wgs, wgf], axis=1).astype(BF16)
    w_kv_t = jnp.concatenate([wk.T, wv.T], axis=0).astype(BF16)
    w_small = jnp.pad(jnp.concatenate([wdt, wf], axis=1), ((0, 0), (0, LANES - SSD_HEADS - FOX_HEADS))).astype(BF16)
    bias_small = jnp.pad(jnp.concatenate([dt_bias[0], b_forget[0]]), (0, LANES - SSD_HEADS - FOX_HEADS))[None, :]
    row = lambda a: a.reshape(1, -1).astype(F32)
    ssd_w = (jnp.pad(conv_w[0], ((0, SUBLANES - CONV_WIDTH), (0, 0))), row(conv_b[0]),
             jnp.pad(-jnp.exp(A_log[0].astype(F32)), (0, LANES - SSD_HEADS))[None, :],
             row(jnp.repeat(D_skip[0], SSD_HEAD_DIM)), row(ssd_norm[0]), w_ssd_out[0].astype(BF16),
             *_ssd_routing_matrices())
    w_fox_b, w_o_b = w_fox_out[0].astype(BF16), w_o[0].astype(BF16)

    xp = x_prompt.reshape(rows_p, D_MODEL)
    x_small = jnp.concatenate([x_sample.reshape(n_s, D_MODEL), jnp.zeros((n_s_pad - n_s + PAD, D_MODEL), F32),
                               meta_tokens.astype(F32)], axis=0)
    rows_s = n_s_pad + BLOCK

    tm_ffn = _pick_tile(rows_p, (512, 256, 128))
    tm_in = _pick_tile(rows_p, (256, 128))

    x1 = _ffn(xp, rows_p, tm_ffn, row(norm_ffn1[0]), *ffn1_w, name="ffn1")
    tm_small = _pick_tile(rows_s, (256, 128))
    x1s = _ffn(x_small, rows_s, tm_small, row(norm_ffn1[0]), *ffn1_w, name="ffn1_small")
    z, xbc, q, gs, gf, k_t, v_t, sm = _inproj(x1, tm_in, seq, row(norm_mix[0]), w_main, w_kv_t, w_small, bias_small,
                                              "inproj")
    zs, xbcs, qs, gss, gfs, ks_t, vs_t, sms = _inproj(x1s, tm_small, rows_s, row(norm_mix[0]), w_main, w_kv_t, w_small,
                                                      bias_small, "inproj_small")
    ks, vs = ks_t[0].T, vs_t[0].T

    o_ssd, st_p = _ssd_prompt(xbc, z, sm, xbcs, sms, meta_blk, ssd_w, batch, seq)

    tq = _pick_tile(seq, (512, 256, 128))
    c_tok, c_meta = _logf_cumsum(sm, sms, meta_blk, batch, seq, _pick_tile(seq, (512, 256, 128)))
    lo, hi = SSD_HEADS, SSD_HEADS + FOX_HEADS
    o_att = _fox_prompt(q, k_t, v_t, c_tok, ks_t, vs_t, c_meta, meta_blk, batch, seq, tq)

    y_p = _ffn(x1, rows_p, tm_ffn, row(norm_ffn2[0]), *ffn2_w, merge_args=(o_ssd, o_att, gs, gf, w_fox_b, w_o_b),
               final_norm_w=row(norm_final), name="merge_ffn2")

    pad8 = lambda a: jnp.pad(a[:n_s].reshape(nb, n_new, -1), ((0, 0), (0, SUBLANES - n_new), (0, 0)))
    hist8 = jnp.pad(state_conv[0].astype(F32), ((0, 0), (SUBLANES - (CONV_WIDTH - 1), 0), (0, 0)))
    rper = SSD_HEADS // SSD_GROUPS
    state0 = state_ssm[0].astype(F32).reshape(nb, SSD_GROUPS, rper, SSD_HEAD_DIM, SSD_STATE)
    state0 = state0.transpose(0, 1, 4, 2, 3).reshape(nb, SSD_GROUPS, SSD_STATE, GROUP_W)
    o_ssd_s8, st_s = _ssd_sample(pad8(xbcs), pad8(zs), pad8(sms), hist8, state0, ssd_w, n_new)

    lf_t = cache_logf[0].astype(F32).transpose(0, 2, 1).reshape(n_phys * FOX_HEADS, page)
    c_page, tot_page = _page_cumsum(lf_t, _pick_tile(n_phys * FOX_HEADS, (4096, 2048, 1024, 512, 256, 128, 16)))
    c_page = c_page.reshape(n_phys, FOX_HEADS, page)
    tot_page = tot_page.reshape(n_phys, FOX_HEADS, page)
    lf_new_t = jnp.pad(sms[:n_s, lo:hi].reshape(nb, n_new, FOX_HEADS).transpose(0, 2, 1),
                       ((0, 0), (0, 0), (0, LANES - n_new)))
    pps = _pick_tile(page_table.shape[1], (16, 8, 4, 2, 1))
    o_att_s = _fox_sample(page_table, qs[:n_s].reshape(nb, n_new, FOX_INNER), pad8(ks), pad8(vs), lf_new_t,
                          cache_k.transpose(0, 1, 3, 4, 2), cache_v.transpose(0, 1, 3, 4, 2), c_page, tot_page, pps)

    rows_pad = lambda a: jnp.pad(a.reshape(n_s, -1), ((0, n_s_pad - n_s), (0, 0)))
    y_s = _ffn(x1s, n_s_pad, BLOCK, row(norm_ffn2[0]), *ffn2_w,
               merge_args=(rows_pad(o_ssd_s8[:, :n_new]), rows_pad(o_att_s).astype(BF16), gss, gfs, w_fox_b, w_o_b),
               final_norm_w=row(norm_final), name="merge_ffn2_small")

    meta_rows = slice(n_s_pad + PAD, n_s_pad + BLOCK)

    def with_meta(tok, small, width):
        m = jnp.broadcast_to(small[meta_rows][None], (batch, N_META, width))
        return jnp.concatenate([m, tok.reshape(batch, seq, width)], axis=1)

    def unpack_state(st, n):
        st = st.reshape(n, SSD_GROUPS, SSD_STATE, rper, SSD_HEAD_DIM).transpose(0, 1, 3, 4, 2)
        return st.reshape(1, n, SSD_HEADS, SSD_HEAD_DIM, SSD_STATE)

    y_prompt = y_p.reshape(batch, seq, D_MODEL)
    y_sample = y_s[:n_s].reshape(nb, n_new, D_MODEL)
    def with_meta_t(tok_t, small_t):
        m = jnp.broadcast_to(small_t[:, :, meta_rows], (batch, FOX_INNER, N_META))
        full = jnp.concatenate([m, tok_t], axis=2).reshape(batch, FOX_HEADS, FOX_HEAD_DIM, seq + N_META)
        return full.transpose(0, 3, 1, 2)[None]

    k_p = with_meta_t(k_t, ks_t)
    v_p = with_meta_t(v_t, vs_t)
    lf_p = with_meta(sm[:, lo:hi], sms[:, lo:hi], FOX_HEADS)[None]
    conv_p = xbc.reshape(batch, seq, CONV_DIM)[:, seq - (CONV_WIDTH - 1):][None]
    ssm_p = unpack_state(st_p, batch)
    k_s = ks[:n_s].reshape(1, nb, n_new, FOX_HEADS, FOX_HEAD_DIM)
    v_s = vs[:n_s].reshape(1, nb, n_new, FOX_HEADS, FOX_HEAD_DIM)
    lf_s = sms[:n_s, lo:hi].reshape(1, nb, n_new, FOX_HEADS)
    conv_s = jnp.concatenate([state_conv[0].astype(F32), xbcs[:n_s].reshape(nb, n_new, CONV_DIM)], axis=1)[:, n_new:][None]
    ssm_s = unpack_state(st_s, nb)
    return (y_prompt, y_sample, k_p, v_p, lf_p, conv_p, ssm_p, k_s, v_s, lf_s, conv_s, ssm_s)
```

```python
import functools

import jax
import jax.numpy as jnp
from jax import lax
from jax.experimental import pallas as pl
from jax.experimental.pallas import tpu as pltpu

F32 = jnp.float32
BF16 = jnp.bfloat16

D_MODEL = 1024
D_FF = 2816
NORM_EPS = 1e-6
N_META = 16
BLOCK = 128
PAD = BLOCK - N_META
SSD_HEADS = 16
SSD_HEAD_DIM = 64
SSD_INNER = SSD_HEADS * SSD_HEAD_DIM
SSD_STATE = 128
SSD_GROUPS = 2
GROUP_W = SSD_INNER // SSD_GROUPS
CONV_WIDTH = 4
CONV_DIM = SSD_INNER + 2 * SSD_GROUPS * SSD_STATE
FOX_HEADS = 16
FOX_HEAD_DIM = 64
FOX_INNER = FOX_HEADS * FOX_HEAD_DIM
FOX_SCALE = FOX_HEAD_DIM ** -0.5
NEG_INF = -1e30

LANES = 128
SUBLANES = 8
V7X_VMEM_BYTES = 64 * 1024 * 1024

FF_CHUNK = D_FF // 2
LOG2E = 1.4426950408889634
Q_SCALE = FOX_SCALE * LOG2E
HIST = SUBLANES


def _dot(a, b):
    return jnp.dot(a, b, preferred_element_type=F32)


def _dot_nt(a, b):
    return lax.dot_general(a, b, (((1,), (1,)), ((), ())), preferred_element_type=F32)


def _split3(x):
    hi = x.astype(BF16)
    r = x - hi.astype(F32)
    mid = r.astype(BF16)
    lo = (r - mid.astype(F32)).astype(BF16)
    return hi, mid, lo


def _dot_exact(a01, x):
    hi, mid, lo = _split3(x)
    return _dot(a01, hi) + _dot(a01, mid) + _dot(a01, lo)


def _dot_exact_r(x, b01):
    hi, mid, lo = _split3(x)
    return _dot(hi, b01) + _dot(mid, b01) + _dot(lo, b01)


def _transpose_exact(x, eye):
    hi, mid, lo = _split3(x)
    return _dot_nt(eye, hi) + _dot_nt(eye, mid) + _dot_nt(eye, lo)


def _tril(n):
    return (lax.broadcasted_iota(jnp.int32, (n, n), 1) <= lax.broadcasted_iota(jnp.int32, (n, n), 0)).astype(BF16)


def _triu(n):
    return (lax.broadcasted_iota(jnp.int32, (n, n), 0) <= lax.broadcasted_iota(jnp.int32, (n, n), 1)).astype(BF16)


def _rms(x):
    return x * lax.rsqrt(jnp.mean(x * x, axis=-1, keepdims=True) + NORM_EPS)


def _silu(x):
    return x * jax.nn.sigmoid(x)


def _params(n_axes, vmem_mib):
    return pltpu.CompilerParams(dimension_semantics=("arbitrary",) * n_axes,
                                vmem_limit_bytes=min(vmem_mib * 1024 * 1024, V7X_VMEM_BYTES - 4 * 1024 * 1024))


def _resident(shape):
    nd = len(shape)
    return pl.BlockSpec(shape, lambda *_: (0,) * nd, pipeline_mode=pl.Buffered(1))


def _ffn_body(*refs, merge, final_norm):
    refs = list(refs)
    x_ref = refs.pop(0)
    if merge:
        os_ref, oa_ref, gs_ref, gf_ref, wf_ref, wo_ref = refs[:6]
        refs = refs[6:]
    nw_ref, wg_ref, wu_ref, wd_ref = refs[:4]
    refs = refs[4:]
    if final_norm:
        fnw_ref = refs.pop(0)
    o_ref, hn_ref = refs

    x = x_ref[...]
    if merge:
        o_fox = _dot(oa_ref[...], wf_ref[...])
        mix = jax.nn.sigmoid(gs_ref[...]) * os_ref[...] + jax.nn.sigmoid(gf_ref[...]) * o_fox
        x = x + _dot(mix.astype(BF16), wo_ref[...])
    hn_ref[...] = (_rms(x) * nw_ref[...]).astype(BF16)
    y = x
    for c in range(D_FF // FF_CHUNK):
        cols = slice(c * FF_CHUNK, (c + 1) * FF_CHUNK)
        g = _dot(hn_ref[...], wg_ref[:, cols])
        u = _dot(hn_ref[...], wu_ref[:, cols])
        y = y + 0.5 * _dot((_silu(g) * u).astype(BF16), wd_ref[cols, :])
    if final_norm:
        y = _rms(y) * fnw_ref[...]
    o_ref[...] = y


def _ffn(x, n_rows, tm, norm_w, wg, wu, wd, merge_args=None, final_norm_w=None, name="ffn"):
    merge = merge_args is not None
    row_spec = pl.BlockSpec((tm, D_MODEL), lambda i: (i, 0))
    args, specs = [x], [row_spec]
    if merge:
        o_ssd, o_att, g_ssd, g_fox, w_fox, w_o = merge_args
        args += [o_ssd, o_att, g_ssd, g_fox, w_fox, w_o]
        specs += [row_spec] * 4 + [_resident(w_fox.shape), _resident(w_o.shape)]
    args += [norm_w, wg, wu, wd]
    specs += [_resident(norm_w.shape), _resident(wg.shape), _resident(wu.shape), _resident(wd.shape)]
    if final_norm_w is not None:
        args.append(final_norm_w)
        specs.append(_resident(final_norm_w.shape))
    return pl.pallas_call(
        functools.partial(_ffn_body, merge=merge, final_norm=final_norm_w is not None),
        grid=(n_rows // tm,),
        in_specs=specs,
        out_specs=row_spec,
        out_shape=jax.ShapeDtypeStruct((n_rows, D_MODEL), F32),
        scratch_shapes=[pltpu.VMEM((tm, D_MODEL), BF16)],
        compiler_params=_params(1, 60),
        name=name,
    )(*args)


_MAIN_WIDTHS = (SSD_INNER, CONV_DIM, FOX_INNER, D_MODEL, D_MODEL)
_Q_INDEX = 2


def _inproj_body(x_ref, nw_ref, wm_ref, wkv_ref, ws_ref, bias_ref, *refs):
    n_main = len(_MAIN_WIDTHS)
    outs, (kt_ref, vt_ref, sm_ref, hn_ref) = refs[:n_main], refs[n_main:]
    hn_ref[...] = (_rms(x_ref[...]) * nw_ref[...]).astype(BF16)
    off = 0
    for i, (ref, w) in enumerate(zip(outs, _MAIN_WIDTHS)):
        val = _dot(hn_ref[...], wm_ref[:, off:off + w])
        ref[...] = (val * Q_SCALE).astype(BF16) if i == _Q_INDEX else val
        off += w
    kt_ref[...] = _dot_nt(wkv_ref[0:FOX_INNER, :], hn_ref[...])
    vt_ref[...] = _dot_nt(wkv_ref[FOX_INNER:, :], hn_ref[...])
    raw = _dot(hn_ref[...], ws_ref[...]) + bias_ref[...]
    t = jnp.log1p(jnp.exp(-jnp.abs(raw)))
    lane = lax.broadcasted_iota(jnp.int32, raw.shape, 1)
    sm_ref[...] = jnp.where(lane < SSD_HEADS, jnp.maximum(raw, 0.0) + t,
                            jnp.where(lane < SSD_HEADS + FOX_HEADS, jnp.minimum(raw, 0.0) - t, 0.0))


def _inproj(x, tm, seq_rows, norm_w, w_main, w_kv_t, w_small, bias_small, name):
    rows = x.shape[0]
    nseq, per_seq = rows // seq_rows, seq_rows // tm
    row = lambda w: pl.BlockSpec((tm, w), lambda i: (i, 0))
    row_shape = lambda i, w: jax.ShapeDtypeStruct((rows, w), BF16 if i == _Q_INDEX else F32)
    t_spec = pl.BlockSpec((None, FOX_INNER, tm), lambda i: (i // per_seq, 0, i % per_seq))
    t_shape = jax.ShapeDtypeStruct((nseq, FOX_INNER, seq_rows), F32)
    return pl.pallas_call(
        _inproj_body,
        grid=(rows // tm,),
        in_specs=[row(D_MODEL), _resident(norm_w.shape), _resident(w_main.shape), _resident(w_kv_t.shape),
                  _resident(w_small.shape), _resident(bias_small.shape)],
        out_specs=[row(w) for w in _MAIN_WIDTHS] + [t_spec, t_spec, row(LANES)],
        out_shape=[row_shape(i, w) for i, w in enumerate(_MAIN_WIDTHS)] + [t_shape, t_shape, row_shape(-1, LANES)],
        scratch_shapes=[pltpu.VMEM((tm, D_MODEL), BF16)],
        compiler_params=_params(1, 60),
        name=name,
    )(x, norm_w, w_main, w_kv_t, w_small, bias_small)


def _cumsum_rows(x):
    return _dot_exact(_tril(x.shape[0]), x)


def _logf_cumsum_body(lt_ref, lm_ref, ct_ref, cm_ref, carry_ref):
    @pl.when(pl.program_id(1) == 0)
    def _():
        row = lax.broadcasted_iota(jnp.int32, (BLOCK, LANES), 0)
        cm = _cumsum_rows(jnp.where(row >= PAD, lm_ref[...], 0.0))
        cm_ref[...] = cm
        carry_ref[...] = cm[BLOCK - 1:BLOCK, :]

    c = _cumsum_rows(lt_ref[...]) + carry_ref[...]
    ct_ref[...] = c
    carry_ref[...] = c[c.shape[0] - 1:, :]


def _logf_cumsum(sm_tok, sm_small, meta_blk, batch, seq, tb):
    nt = seq // tb
    return pl.pallas_call(
        _logf_cumsum_body,
        grid=(batch, nt),
        in_specs=[pl.BlockSpec((tb, LANES), lambda b, j: (b * nt + j, 0)),
                  pl.BlockSpec((BLOCK, LANES), lambda b, j: (meta_blk, 0))],
        out_specs=[pl.BlockSpec((tb, LANES), lambda b, j: (b * nt + j, 0)),
                   pl.BlockSpec((None, BLOCK, LANES), lambda b, j: (b, 0, 0))],
        out_shape=[jax.ShapeDtypeStruct((batch * seq, LANES), F32),
                   jax.ShapeDtypeStruct((batch, BLOCK, LANES), F32)],
        scratch_shapes=[pltpu.VMEM((1, LANES), F32)],
        compiler_params=_params(2, 32),
        name="logf_cumsum",
    )(sm_tok, sm_small)


_BIAS_PIECES = 3
_Q_GROUP = 512


def _fox_prompt_body(q_ref, kt_ref, vt_ref, ct_ref, km_ref, vm_ref, cm_ref, o_ref,
                     k0_sc, k1_sc, v0_sc, v1_sc, km0_sc, km1_sc, vm0_sc, vm1_sc, sa_sc, sb_sc, *, tq):
    hp = pl.program_id(1)
    qi = pl.program_id(2)
    lane = lax.broadcasted_iota(jnp.int32, (1, LANES), 1)
    is0 = lane < FOX_HEAD_DIM
    extra0, extra1 = FOX_HEAD_DIM, 0

    @pl.when(qi == 0)
    def _():
        src = lax.broadcasted_iota(jnp.int32, (_BIAS_PIECES * LANES, LANES), 0)
        dst = lax.broadcasted_iota(jnp.int32, (_BIAS_PIECES * LANES, LANES), 1)
        c_lane = SSD_HEADS + 2 * hp

        def route(head, base):
            return jnp.where((src % LANES == c_lane + head) & (dst == base + src // LANES), 1.0, 0.0).astype(BF16)

        route0, route1 = route(0, extra0), route(1, extra1)
        sub = lax.broadcasted_iota(jnp.int32, (LANES, 1), 0)
        top_rows = sub < FOX_HEAD_DIM
        one_row0 = jnp.where(sub == extra0, 1.0, 0.0)
        one_row1 = jnp.where(sub == extra1, 1.0, 0.0)

        def eye_of(n):
            return (lax.broadcasted_iota(jnp.int32, (n, n), 0) == lax.broadcasted_iota(jnp.int32, (n, n), 1)).astype(BF16)

        def prep(kt, vt, bias, eye):
            k = _dot_nt(eye, kt.astype(BF16))
            pieces = jnp.concatenate(_split3(bias), axis=1)
            k0 = (jnp.where(is0, k, 0.0) + _dot(pieces, route0)).astype(BF16)
            k1 = (jnp.where(is0, 0.0, k) + _dot(pieces, route1)).astype(BF16)
            v0 = jnp.where(top_rows, vt, one_row0).astype(BF16)
            v1 = jnp.where(top_rows, one_row1, vt).astype(BF16)
            return k0, k1, v0, v1

        row = lax.broadcasted_iota(jnp.int32, (BLOCK, LANES), 0)
        bias_m = jnp.where(row >= PAD, cm_ref[...] * (-LOG2E), NEG_INF)
        km0_sc[...], km1_sc[...], vm0_sc[...], vm1_sc[...] = prep(km_ref[...], vm_ref[...], bias_m, eye_of(BLOCK))
        eye_q = eye_of(tq)
        for i in range(k0_sc.shape[0]):
            blk = slice(i * tq, (i + 1) * tq)
            k0_sc[i], k1_sc[i], v0_sc[i], v1_sc[i] = prep(kt_ref[:, blk], vt_ref[:, blk],
                                                         ct_ref[blk, :] * (-LOG2E), eye_q)

    q = q_ref[...].astype(F32)
    ones0 = jnp.where((lane >= extra0) & (lane < extra0 + _BIAS_PIECES), 1.0, 0.0)
    ones1 = jnp.where((lane >= extra1) & (lane < extra1 + _BIAS_PIECES), 1.0, 0.0)
    q0 = jnp.where(is0, q, ones0).astype(BF16)
    q1 = jnp.where(is0, ones1, q).astype(BF16)

    def consume(m, acc, s, vt, mask):
        if mask is not None:
            s = jnp.where(mask, s, NEG_INF)
        m_new = jnp.maximum(m, jnp.max(s, axis=0, keepdims=True))
        p = jnp.exp2(s - m_new).astype(BF16)
        return m_new, acc * jnp.exp2(m - m_new) + _dot(vt, p)

    def consume_last(m, acc, s, vt, qh, km_ref_, vm_ref_, mask):
        s = jnp.where(mask, s, NEG_INF)
        s_meta = _dot_nt(km_ref_[PAD:, :], qh)
        m_new = jnp.maximum(m, jnp.maximum(jnp.max(s, axis=0, keepdims=True), jnp.max(s_meta, axis=0, keepdims=True)))
        p = jnp.exp2(s - m_new).astype(BF16)
        p_meta = jnp.concatenate([jnp.zeros((PAD, tq), BF16), jnp.exp2(s_meta - m_new).astype(BF16)], axis=0)
        return m_new, acc * jnp.exp2(m - m_new) + _dot(vt, p) + _dot(vm_ref_[...], p_meta)

    neg = jnp.full((1, tq), NEG_INF, F32)
    zero = jnp.zeros((LANES, tq), F32)

    def issue(j, s_ref):
        s_ref[0] = _dot_nt(k0_sc[j], q0)
        s_ref[1] = _dot_nt(k1_sc[j], q1)

    def retire(j, s_ref, carry, mask=None):
        m0, acc0, m1, acc1 = carry
        m0, acc0 = consume(m0, acc0, s_ref[0], v0_sc[j], mask)
        m1, acc1 = consume(m1, acc1, s_ref[1], v1_sc[j], mask)
        return m0, acc0, m1, acc1

    def pair(i, carry):
        issue(2 * i + 1, sb_sc)
        carry = retire(2 * i, sa_sc, carry)
        issue(2 * i + 2, sa_sc)
        return retire(2 * i + 1, sb_sc, carry)

    issue(0, sa_sc)
    carry = lax.fori_loop(0, qi // 2, pair, (neg, zero, neg, zero))
    causal = lax.broadcasted_iota(jnp.int32, (tq, 1), 0) <= lax.broadcasted_iota(jnp.int32, (1, tq), 1)
    top = lax.broadcasted_iota(jnp.int32, (LANES, 1), 0) < FOX_HEAD_DIM

    def finish(s_ref, carry):
        m0, acc0, m1, acc1 = carry
        _, acc0 = consume_last(m0, acc0, s_ref[0], v0_sc[qi], q0, km0_sc, vm0_sc, causal)
        _, acc1 = consume_last(m1, acc1, s_ref[1], v1_sc[qi], q1, km1_sc, vm1_sc, causal)
        o_t = jnp.where(top, acc0 / acc0[extra0:extra0 + 1, :], acc1 / acc1[extra1:extra1 + 1, :])
        o_ref[...] = o_t.T.astype(BF16)

    @pl.when(qi % 2 == 0)
    def _():
        finish(sa_sc, carry)

    @pl.when(qi % 2 == 1)
    def _():
        issue(qi, sb_sc)
        finish(sb_sc, retire(qi - 1, sa_sc, carry))


def _fox_prompt(q, k_t, v_t, c_tok, k_small_t, v_small_t, c_meta, meta_blk, batch, seq, tq):
    nq = seq // tq
    npair = FOX_INNER // LANES
    kv_spec = pl.BlockSpec((None, LANES, seq), lambda b, hp, i: (b, hp, 0))
    meta_spec = pl.BlockSpec((None, LANES, BLOCK), lambda b, hp, i: (0, hp, meta_blk))
    blk_spec = pl.BlockSpec((tq, LANES), lambda b, hp, i: (b * nq + i, hp))
    return pl.pallas_call(
        functools.partial(_fox_prompt_body, tq=tq),
        grid=(batch, npair, nq),
        in_specs=[blk_spec, kv_spec, kv_spec, pl.BlockSpec((seq, LANES), lambda b, hp, i: (b, 0)),
                  meta_spec, meta_spec, pl.BlockSpec((None, BLOCK, LANES), lambda b, hp, i: (b, 0, 0))],
        out_specs=blk_spec,
        out_shape=jax.ShapeDtypeStruct((batch * seq, FOX_INNER), BF16),
        scratch_shapes=[pltpu.VMEM((nq, tq, LANES), BF16)] * 2 + [pltpu.VMEM((nq, LANES, tq), BF16)] * 2
                       + [pltpu.VMEM((BLOCK, LANES), BF16)] * 4 + [pltpu.VMEM((2, tq, tq), F32)] * 2,
        compiler_params=_params(3, 48),
        name="fox_prompt",
    )(q, k_t, v_t, c_tok, k_small_t, v_small_t, c_meta)


def _ssd_chunk(xbuf_ref, dt, z, st_ref, ybuf_ref, w_refs):
    L = BLOCK
    cw_ref, cb_ref, a_ref, dskip_ref, nw_ref, wout_ref, tril_ref, eye_ref, e_head_ref = w_refs
    eye = eye_ref[...]
    conv = cb_ref[...]
    for w in range(CONV_WIDTH):
        conv = conv + xbuf_ref[HIST - (CONV_WIDTH - 1) + w:HIST - (CONV_WIDTH - 1) + w + L, :] * cw_ref[w:w + 1, :]
    xc = _silu(conv)
    xs = xc[:, :SSD_INNER]
    gn = SSD_GROUPS * SSD_STATE

    a_col = _dot_exact(tril_ref[...], dt * a_ref[...])
    a_row = _transpose_exact(a_col, eye)
    dt_row = _transpose_exact(dt, eye)
    e_head = e_head_ref[...]
    a_exp = _dot_exact_r(a_col, e_head)
    decay_in = jnp.exp(a_exp)
    a_last = a_exp[L - 1:L, :]
    to_end = jnp.exp(a_last - a_exp) * _dot_exact_r(dt, e_head)
    xw = (xs * to_end).astype(BF16)
    chunk_decay = decay_in[L - 1:L, :]

    causal = lax.broadcasted_iota(jnp.int32, (L, L), 1) <= lax.broadcasted_iota(jnp.int32, (L, L), 0)
    is0 = lax.broadcasted_iota(jnp.int32, (1, LANES), 1) < SSD_HEAD_DIM
    heads_per_group = SSD_HEADS // SSD_GROUPS
    for g in range(SSD_GROUPS):
        b_g = xc[:, SSD_INNER + g * SSD_STATE:SSD_INNER + (g + 1) * SSD_STATE].astype(BF16)
        c_g = xc[:, SSD_INNER + gn + g * SSD_STATE:SSD_INNER + gn + (g + 1) * SSD_STATE].astype(BF16)
        cb = _dot_nt(c_g, b_g)
        lo, hi = g * GROUP_W, (g + 1) * GROUP_W
        h_in = st_ref[g]
        y_off = _dot(c_g, h_in.astype(BF16)) * decay_in[:, lo:hi]
        ybuf_ref[:, lo:hi] = y_off
        b_t = _dot_nt(eye, b_g).astype(BF16)
        st_ref[g] = h_in * chunk_decay[:, lo:hi] + _dot(b_t, xw[:, lo:hi])
        for pair in range(heads_per_group // 2):
            h0 = g * heads_per_group + 2 * pair
            c0 = h0 * SSD_HEAD_DIM
            x_pair = xs[:, c0:c0 + LANES]
            y_pair = jnp.zeros((L, LANES), F32)
            for k, xm in ((0, jnp.where(is0, x_pair, 0.0)), (1, jnp.where(is0, 0.0, x_pair))):
                h = h0 + k
                seg = a_col[:, h:h + 1] - a_row[h:h + 1, :]
                wgt = cb * jnp.exp(jnp.where(causal, seg, -jnp.inf)) * dt_row[h:h + 1, :]
                y_pair = y_pair + _dot(wgt.astype(BF16), xm.astype(BF16))
            ybuf_ref[:, c0:c0 + LANES] += y_pair

    y = (ybuf_ref[...] + xs * dskip_ref[...]) * _silu(z)
    yn = (_rms(y) * nw_ref[...]).astype(BF16)
    return _dot(yn, wout_ref[...])


_SSD_WEIGHT_SHAPES = ((SUBLANES, CONV_DIM), (1, CONV_DIM), (1, LANES), (1, SSD_INNER), (1, SSD_INNER),
                      (SSD_INNER, D_MODEL),
                      (BLOCK, BLOCK), (BLOCK, BLOCK), (LANES, SSD_INNER))


def _ssd_routing_matrices():
    r = jnp.arange(LANES)[:, None]
    tril = (jnp.arange(BLOCK)[None, :] <= jnp.arange(BLOCK)[:, None]).astype(BF16)
    eye = jnp.eye(BLOCK, dtype=BF16)
    e_head = (jnp.arange(SSD_INNER)[None, :] // SSD_HEAD_DIM == r).astype(BF16)
    return tril, eye, e_head


def _ssd_prompt_body(xt_ref, zt_ref, st_ref_in, xm_ref, sm_ref, *rest):
    w_refs = rest[:len(_SSD_WEIGHT_SHAPES)]
    o_ref, state_ref, xbuf_ref, st_ref, ybuf_ref = rest[len(_SSD_WEIGHT_SHAPES):]
    c = pl.program_id(1)
    lane = lax.broadcasted_iota(jnp.int32, (BLOCK, LANES), 1)
    row = lax.broadcasted_iota(jnp.int32, (BLOCK, LANES), 0)

    @pl.when(c == 0)
    def _():
        st_ref[...] = jnp.zeros_like(st_ref)
        xbuf_ref[0:HIST, :] = jnp.zeros((HIST, CONV_DIM), F32)
        xbuf_ref[HIST:, :] = xm_ref[...]

    @pl.when(c > 0)
    def _():
        xbuf_ref[HIST:, :] = xt_ref[...]

    dt = jnp.where(c == 0,
                   jnp.where((lane < SSD_HEADS) & (row >= PAD), sm_ref[...], 0.0),
                   jnp.where(lane < SSD_HEADS, st_ref_in[...], 0.0))
    o_ref[...] = _ssd_chunk(xbuf_ref, dt, zt_ref[...], st_ref, ybuf_ref, w_refs)
    xbuf_ref[0:HIST, :] = xbuf_ref[BLOCK:BLOCK + HIST, :]

    @pl.when(c == pl.num_programs(1) - 1)
    def _():
        state_ref[...] = st_ref[...]


def _ssd_prompt(xbc, z, sm, xbc_small, sm_small, meta_blk, weights, batch, seq):
    nc = seq // BLOCK
    tok = lambda w: pl.BlockSpec((BLOCK, w), lambda b, c: (b * nc + jnp.maximum(c - 1, 0), 0))
    meta = lambda w: pl.BlockSpec((BLOCK, w), lambda b, c: (meta_blk, 0))
    return pl.pallas_call(
        _ssd_prompt_body,
        grid=(batch, nc + 1),
        in_specs=[tok(CONV_DIM), tok(SSD_INNER), tok(LANES), meta(CONV_DIM), meta(LANES)]
                 + [_resident(s) for s in _SSD_WEIGHT_SHAPES],
        out_specs=[tok(D_MODEL),
                   pl.BlockSpec((None, SSD_GROUPS, SSD_STATE, GROUP_W), lambda b, c: (b, 0, 0, 0))],
        out_shape=[jax.ShapeDtypeStruct((batch * seq, D_MODEL), F32),
                   jax.ShapeDtypeStruct((batch, SSD_GROUPS, SSD_STATE, GROUP_W), F32)],
        scratch_shapes=[pltpu.VMEM((HIST + BLOCK, CONV_DIM), F32),
                        pltpu.VMEM((SSD_GROUPS, SSD_STATE, GROUP_W), F32),
                        pltpu.VMEM((BLOCK, SSD_INNER), F32)],
        compiler_params=_params(2, 40),
        name="ssd_prompt",
    )(xbc, z, sm, xbc_small, sm_small, *weights)


def _ssd_sample_body(x_ref, z_ref, sm_ref, hist_ref, state0_ref, *rest, n_new):
    w_refs = rest[:len(_SSD_WEIGHT_SHAPES)]
    o_ref, state_ref, xbuf_ref, st_ref, ybuf_ref, dt_ref, zbuf_ref = rest[len(_SSD_WEIGHT_SHAPES):]
    rows = x_ref.shape[0]
    xbuf_ref[0:HIST, :] = hist_ref[...]
    xbuf_ref[HIST:HIST + rows, :] = x_ref[...]
    xbuf_ref[HIST + rows:, :] = jnp.zeros((BLOCK - rows, CONV_DIM), F32)
    lane = lax.broadcasted_iota(jnp.int32, (rows, LANES), 1)
    row = lax.broadcasted_iota(jnp.int32, (rows, LANES), 0)
    dt_ref[...] = jnp.zeros_like(dt_ref)
    dt_ref[0:rows, :] = jnp.where((lane < SSD_HEADS) & (row < n_new), sm_ref[...], 0.0)
    zbuf_ref[...] = jnp.zeros_like(zbuf_ref)
    zbuf_ref[0:rows, :] = z_ref[...]
    st_ref[...] = state0_ref[...]
    out = _ssd_chunk(xbuf_ref, dt_ref[...], zbuf_ref[...], st_ref, ybuf_ref, w_refs)
    o_ref[...] = out[0:rows, :]
    state_ref[...] = st_ref[...]


def _ssd_sample(x8, z8, sm8, hist8, state0, weights, n_new):
    nb, rows = x8.shape[0], x8.shape[1]
    per = lambda w: pl.BlockSpec((None, rows, w), lambda b: (b, 0, 0))
    st_spec = pl.BlockSpec((None, SSD_GROUPS, SSD_STATE, GROUP_W), lambda b: (b, 0, 0, 0))
    return pl.pallas_call(
        functools.partial(_ssd_sample_body, n_new=n_new),
        grid=(nb,),
        in_specs=[per(CONV_DIM), per(SSD_INNER), per(LANES), per(CONV_DIM), st_spec]
                 + [_resident(s) for s in _SSD_WEIGHT_SHAPES],
        out_specs=[per(D_MODEL), st_spec],
        out_shape=[jax.ShapeDtypeStruct((nb, rows, D_MODEL), F32),
                   jax.ShapeDtypeStruct((nb, SSD_GROUPS, SSD_STATE, GROUP_W), F32)],
        scratch_shapes=[pltpu.VMEM((HIST + BLOCK, CONV_DIM), F32),
                        pltpu.VMEM((SSD_GROUPS, SSD_STATE, GROUP_W), F32),
                        pltpu.VMEM((BLOCK, SSD_INNER), F32),
                        pltpu.VMEM((BLOCK, LANES), F32),
                        pltpu.VMEM((BLOCK, SSD_INNER), F32)],
        compiler_params=_params(1, 40),
        name="ssd_sample",
    )(x8, z8, sm8, hist8, state0, *weights)


def _page_cumsum_body(x_ref, c_ref, tot_ref):
    both = jnp.concatenate([_triu(LANES), jnp.ones((LANES, LANES), BF16)], axis=1)
    r = _dot_exact_r(x_ref[...], both)
    c_ref[...] = r[:, :LANES]
    tot_ref[...] = r[:, LANES:]


def _page_cumsum(lf_t, tr):
    rows = lf_t.shape[0]
    spec = pl.BlockSpec((tr, LANES), lambda i: (i, 0))
    return pl.pallas_call(
        _page_cumsum_body,
        grid=(rows // tr,),
        in_specs=[spec],
        out_specs=[spec, spec],
        out_shape=[jax.ShapeDtypeStruct((rows, LANES), F32)] * 2,
        compiler_params=_params(1, 32),
        name="page_cumsum",
    )(lf_t)


def _fox_sample_body(pt_ref, q_ref, kn_ref, vn_ref, ln_ref, *refs, pps, n_new):
    k_refs, v_refs, c_refs, t_refs = refs[:pps], refs[pps:2 * pps], refs[2 * pps:3 * pps], refs[3 * pps:4 * pps]
    o_ref, qbd_sc, m_sc, l_sc, acc_sc, carry_sc, kn_sc, vn_sc = refs[4 * pps:]
    del pt_ref
    j = pl.program_id(1)
    nrow = n_new * FOX_HEADS

    @pl.when(j == 0)
    def _():
        col_head = lax.broadcasted_iota(jnp.int32, (FOX_HEADS, FOX_INNER), 1) // FOX_HEAD_DIM
        own = col_head == lax.broadcasted_iota(jnp.int32, (FOX_HEADS, FOX_INNER), 0)
        q = q_ref[...].astype(F32)
        for t in range(n_new):
            qbd_sc[t * FOX_HEADS:(t + 1) * FOX_HEADS, :] = jnp.where(own, q[t:t + 1, :], 0.0).astype(BF16)
        m_sc[...] = jnp.full_like(m_sc, NEG_INF)
        l_sc[...] = jnp.zeros_like(l_sc)
        acc_sc[...] = jnp.zeros_like(acc_sc)
        carry_sc[...] = jnp.zeros_like(carry_sc)

    qbd = qbd_sc[...]

    def tile_rows(bias16):
        return jnp.concatenate([bias16] * n_new, axis=0)

    def accumulate(s_list, pv_fn_list):
        m_old = m_sc[...]
        m_new = m_old
        for s in s_list:
            m_new = jnp.maximum(m_new, jnp.max(s, axis=1, keepdims=True))
        alpha = jnp.exp2(m_old - m_new)
        l_new = alpha * l_sc[...]
        pv = jnp.zeros(acc_sc.shape, F32)
        for s, pv_fn in zip(s_list, pv_fn_list):
            p = jnp.exp2(s - m_new)
            l_new = l_new + jnp.sum(p, axis=1, keepdims=True)
            pv = pv + pv_fn(p.astype(BF16))
        m_sc[...] = m_new
        l_sc[...] = l_new
        acc_sc[...] = acc_sc[...] * alpha + pv

    carry = carry_sc[...]
    s_list, pv_fns = [], []
    for i in range(pps):
        kt = k_refs[i][...].reshape(FOX_INNER, BLOCK).astype(BF16)
        vt = v_refs[i][...].reshape(FOX_INNER, BLOCK).astype(BF16)
        s_list.append(_dot(qbd, kt) - tile_rows((carry + c_refs[i][...]) * LOG2E))
        pv_fns.append(functools.partial(lambda p, vt: _dot_nt(p, vt), vt=vt))
        carry = carry + t_refs[i][...]
    carry_sc[...] = carry
    accumulate(s_list, pv_fns)

    @pl.when(j == pl.num_programs(1) - 1)
    def _():
        rows = kn_ref.shape[0]
        kn_sc[...] = jnp.zeros_like(kn_sc)
        vn_sc[...] = jnp.zeros_like(vn_sc)
        kn_sc[0:rows, :] = kn_ref[...].astype(BF16)
        vn_sc[0:rows, :] = vn_ref[...].astype(BF16)
        bias = (carry_sc[...] + _dot_exact_r(ln_ref[...], _triu(LANES))) * LOG2E
        s = _dot_nt(qbd, kn_sc[...]) - tile_rows(bias)
        key = lax.broadcasted_iota(jnp.int32, (nrow, BLOCK), 1)
        tok = lax.broadcasted_iota(jnp.int32, (nrow, BLOCK), 0) // FOX_HEADS
        s = jnp.where(key <= tok, s, NEG_INF)
        vn = vn_sc[...]
        accumulate([s], [lambda p: _dot(p, vn)])
        o = acc_sc[...] / l_sc[...]
        row_head = lax.broadcasted_iota(jnp.int32, (nrow, FOX_INNER), 0) % FOX_HEADS
        col_head = lax.broadcasted_iota(jnp.int32, (nrow, FOX_INNER), 1) // FOX_HEAD_DIM
        o = jnp.where(row_head == col_head, o, 0.0)
        o_ref[...] = jnp.sum(o.reshape(n_new, FOX_HEADS, FOX_INNER), axis=1)


def _fox_sample(page_table, q, k_new8, v_new8, lf_new_t, cache_kt, cache_vt, c_page, tot_page, pps):
    nb, n_new = q.shape[0], q.shape[1]
    n_pages = page_table.shape[1]
    rows8 = k_new8.shape[1]
    pt_flat = page_table.reshape(-1)

    def page_idx(i):
        return lambda b, j, pt: pt[b * n_pages + j * pps + i]

    def kv_spec(i):
        f = page_idx(i)
        return pl.BlockSpec((None, None, FOX_HEADS, FOX_HEAD_DIM, BLOCK), lambda b, j, pt: (0, f(b, j, pt), 0, 0, 0))

    def c_spec(i):
        f = page_idx(i)
        return pl.BlockSpec((None, FOX_HEADS, LANES), lambda b, j, pt: (f(b, j, pt), 0, 0))

    per = lambda r, w: pl.BlockSpec((None, r, w), lambda b, j, pt: (b, 0, 0))
    in_specs = [per(n_new, FOX_INNER), per(rows8, FOX_INNER), per(rows8, FOX_INNER), per(FOX_HEADS, LANES)]
    in_specs += [kv_spec(i) for i in range(pps)] * 2 + [c_spec(i) for i in range(pps)] * 2
    nrow = n_new * FOX_HEADS
    grid_spec = pltpu.PrefetchScalarGridSpec(
        num_scalar_prefetch=1,
        grid=(nb, n_pages // pps),
        in_specs=in_specs,
        out_specs=per(n_new, FOX_INNER),
        scratch_shapes=[pltpu.VMEM((nrow, FOX_INNER), BF16), pltpu.VMEM((nrow, 1), F32), pltpu.VMEM((nrow, 1), F32),
                        pltpu.VMEM((nrow, FOX_INNER), F32), pltpu.VMEM((FOX_HEADS, LANES), F32),
                        pltpu.VMEM((BLOCK, FOX_INNER), BF16), pltpu.VMEM((BLOCK, FOX_INNER), BF16)],
    )
    return pl.pallas_call(
        functools.partial(_fox_sample_body, pps=pps, n_new=n_new),
        grid_spec=grid_spec,
        out_shape=jax.ShapeDtypeStruct((nb, n_new, FOX_INNER), F32),
        compiler_params=_params(2, 56),
        name="fox_sample",
    )(pt_flat, q, k_new8, v_new8, lf_new_t, *([cache_kt] * pps), *([cache_vt] * pps), *([c_page] * pps),
      *([tot_page] * pps))


def _ffn_weights(w_gate, w_up, w_down):
    return w_gate.astype(BF16), w_up.astype(BF16), w_down.astype(BF16)


def _pick_tile(n, candidates):
    for t in candidates:
        if n % t == 0:
            return t
    raise ValueError(f"no tile for {n}")


def kernel(x_prompt, x_sample, cache_k, cache_v, cache_logf, page_table, state_conv, state_ssm, meta_tokens, norm_ffn1, w_ffn1_gate, w_ffn1_up, w_ffn1_down, norm_mix, w_in, conv_w, conv_b, dt_bias, A_log, D_skip, ssd_norm, w_ssd_out, b_forget, w_fox_out, w_o, norm_ffn2, w_ffn2_gate, w_ffn2_up, w_ffn2_down, norm_final):
    batch, seq, _ = x_prompt.shape
    nb, n_new, _ = x_sample.shape
    depth, n_phys, page = cache_k.shape[0], cache_k.shape[1], cache_k.shape[2]
    assert depth == 1 and page == BLOCK and seq % BLOCK == 0 and n_new <= SUBLANES
    n_s = nb * n_new
    n_s_pad = -(-n_s // BLOCK) * BLOCK
    meta_blk = n_s_pad // BLOCK
    rows_p = batch * seq

    ffn1_w = _ffn_weights(w_ffn1_gate[0], w_ffn1_up[0], w_ffn1_down[0])
    ffn2_w = _ffn_weights(w_ffn2_gate[0], w_ffn2_up[0], w_ffn2_down[0])
    wz, wxbc, wdt, wq, wk, wv, wf, wgs, wgf = jnp.split(
        w_in[0], [1024, 2560, 2576, 3600, 4624, 5648, 5664, 6688], axis=1)
    w_main = jnp.concatenate([wz, wxbc, wq, wgs, wgf], axis=1).astype(BF16)
    w_kv_t = jnp.concatenate([wk.T, wv.T], axis=0).astype(BF16)
    w_small = jnp.pad(jnp.concatenate([wdt, wf], axis=1), ((0, 0), (0, LANES - SSD_HEADS - FOX_HEADS))).astype(BF16)
    bias_small = jnp.pad(jnp.concatenate([dt_bias[0], b_forget[0]]), (0, LANES - SSD_HEADS - FOX_HEADS))[None, :]
    row = lambda a: a.reshape(1, -1).astype(F32)
    ssd_w = (jnp.pad(conv_w[0], ((0, SUBLANES - CONV_WIDTH), (0, 0))), row(conv_b[0]),
             jnp.pad(-jnp.exp(A_log[0].astype(F32)), (0, LANES - SSD_HEADS))[None, :],
             row(jnp.repeat(D_skip[0], SSD_HEAD_DIM)), row(ssd_norm[0]), w_ssd_out[0].astype(BF16),
             *_ssd_routing_matrices())
    w_fox_b, w_o_b = w_fox_out[0].astype(BF16), w_o[0].astype(BF16)

    xp = x_prompt.reshape(rows_p, D_MODEL)
    x_small = jnp.concatenate([x_sample.reshape(n_s, D_MODEL), jnp.zeros((n_s_pad - n_s + PAD, D_MODEL), F32),
                               meta_tokens.astype(F32)], axis=0)
    rows_s = n_s_pad + BLOCK

    tm_ffn = _pick_tile(rows_p, (512, 256, 128))
    tm_in = _pick_tile(rows_p, (256, 128))

    x1 = _ffn(xp, rows_p, tm_ffn, row(norm_ffn1[0]), *ffn1_w, name="ffn1")
    tm_small = _pick_tile(rows_s, (256, 128))
    x1s = _ffn(x_small, rows_s, tm_small, row(norm_ffn1[0]), *ffn1_w, name="ffn1_small")
    z, xbc, q, gs, gf, k_t, v_t, sm = _inproj(x1, tm_in, seq, row(norm_mix[0]), w_main, w_kv_t, w_small, bias_small,
                                              "inproj")
    zs, xbcs, qs, gss, gfs, ks_t, vs_t, sms = _inproj(x1s, tm_small, rows_s, row(norm_mix[0]), w_main, w_kv_t, w_small,
                                                      bias_small, "inproj_small")
    ks, vs = ks_t[0].T, vs_t[0].T

    o_ssd, st_p = _ssd_prompt(xbc, z, sm, xbcs, sms, meta_blk, ssd_w, batch, seq)

    tq = _pick_tile(seq, (512, 256, 128))
    c_tok, c_meta = _logf_cumsum(sm, sms, meta_blk, batch, seq, _pick_tile(seq, (512, 256, 128)))
    lo, hi = SSD_HEADS, SSD_HEADS + FOX_HEADS
    o_att = _fox_prompt(q, k_t, v_t, c_tok, ks_t, vs_t, c_meta, meta_blk, batch, seq, tq)

    y_p = _ffn(x1, rows_p, tm_ffn, row(norm_ffn2[0]), *ffn2_w, merge_args=(o_ssd, o_att, gs, gf, w_fox_b, w_o_b),
               final_norm_w=row(norm_final), name="merge_ffn2")

    pad8 = lambda a: jnp.pad(a[:n_s].reshape(nb, n_new, -1), ((0, 0), (0, SUBLANES - n_new), (0, 0)))
    hist8 = jnp.pad(state_conv[0].astype(F32), ((0, 0), (SUBLANES - (CONV_WIDTH - 1), 0), (0, 0)))
    rper = SSD_HEADS // SSD_GROUPS
    state0 = state_ssm[0].astype(F32).reshape(nb, SSD_GROUPS, rper, SSD_HEAD_DIM, SSD_STATE)
    state0 = state0.transpose(0, 1, 4, 2, 3).reshape(nb, SSD_GROUPS, SSD_STATE, GROUP_W)
    o_ssd_s8, st_s = _ssd_sample(pad8(xbcs), pad8(zs), pad8(sms), hist8, state0, ssd_w, n_new)

    lf_t = cache_logf[0].astype(F32).transpose(0, 2, 1).reshape(n_phys * FOX_HEADS, page)
    c_page, tot_page = _page_cumsum(lf_t, _pick_tile(n_phys * FOX_HEADS, (4096, 2048, 1024, 512, 256, 128, 16)))
    c_page = c_page.reshape(n_phys, FOX_HEADS, page)
    tot_page = tot_page.reshape(n_phys, FOX_HEADS, page)
    lf_new_t = jnp.pad(sms[:n_s, lo:hi].reshape(nb, n_new, FOX_HEADS).transpose(0, 2, 1),
                       ((0, 0), (0, 0), (0, LANES - n_new)))
    pps = _pick_tile(page_table.shape[1], (16, 8, 4, 2, 1))
    o_att_s = _fox_sample(page_table, qs[:n_s].reshape(nb, n_new, FOX_INNER), pad8(ks), pad8(vs), lf_new_t,
                          cache_k.transpose(0, 1, 3, 4, 2), cache_v.transpose(0, 1, 3, 4, 2), c_page, tot_page, pps)

    rows_pad = lambda a: jnp.pad(a.reshape(n_s, -1), ((0, n_s_pad - n_s), (0, 0)))
    y_s = _ffn(x1s, n_s_pad, BLOCK, row(norm_ffn2[0]), *ffn2_w,
               merge_args=(rows_pad(o_ssd_s8[:, :n_new]), rows_pad(o_att_s).astype(BF16), gss, gfs, w_fox_b, w_o_b),
               final_norm_w=row(norm_final), name="merge_ffn2_small")

    meta_rows = slice(n_s_pad + PAD, n_s_pad + BLOCK)

    def with_meta(tok, small, width):
        m = jnp.broadcast_to(small[meta_rows][None], (batch, N_META, width))
        return jnp.concatenate([m, tok.reshape(batch, seq, width)], axis=1)

    def unpack_state(st, n):
        st = st.reshape(n, SSD_GROUPS, SSD_STATE, rper, SSD_HEAD_DIM).transpose(0, 1, 3, 4, 2)
        return st.reshape(1, n, SSD_HEADS, SSD_HEAD_DIM, SSD_STATE)

    y_prompt = y_p.reshape(batch, seq, D_MODEL)
    y_sample = y_s[:n_s].reshape(nb, n_new, D_MODEL)
    def with_meta_t(tok_t, small_t):
        m = jnp.broadcast_to(small_t[:, :, meta_rows], (batch, FOX_INNER, N_META))
        full = jnp.concatenate([m, tok_t], axis=2).reshape(batch, FOX_HEADS, FOX_HEAD_DIM, seq + N_META)
        return full.transpose(0, 3, 1, 2)[None]

    k_p = with_meta_t(k_t, ks_t)
    v_p = with_meta_t(v_t, vs_t)
    lf_p = with_meta(sm[:, lo:hi], sms[:, lo:hi], FOX_HEADS)[None]
    conv_p = xbc.reshape(batch, seq, CONV_DIM)[:, seq - (CONV_WIDTH - 1):][None]
    ssm_p = unpack_state(st_p, batch)
    k_s = ks[:n_s].reshape(1, nb, n_new, FOX_HEADS, FOX_HEAD_DIM)
    v_s = vs[:n_s].reshape(1, nb, n_new, FOX_HEADS, FOX_HEAD_DIM)
    lf_s = sms[:n_s, lo:hi].reshape(1, nb, n_new, FOX_HEADS)
    conv_s = jnp.concatenate([state_conv[0].astype(F32), xbcs[:n_s].reshape(nb, n_new, CONV_DIM)], axis=1)[:, n_new:][None]
    ssm_s = unpack_state(st_s, nb)
    return (y_prompt, y_sample, k_p, v_p, lf_p, conv_p, ssm_p, k_s, v_s, lf_s, conv_s, ssm_s)
```

```python
import functools

import jax
import jax.numpy as jnp
from jax import lax
from jax.experimental import pallas as pl
from jax.experimental.pallas import tpu as pltpu

F32 = jnp.float32
BF16 = jnp.bfloat16

D_MODEL = 1024
D_FF = 2816
NORM_EPS = 1e-6
N_META = 16
BLOCK = 128
PAD = BLOCK - N_META
SSD_HEADS = 16
SSD_HEAD_DIM = 64
SSD_INNER = SSD_HEADS * SSD_HEAD_DIM
SSD_STATE = 128
SSD_GROUPS = 2
GROUP_W = SSD_INNER // SSD_GROUPS
CONV_WIDTH = 4
CONV_DIM = SSD_INNER + 2 * SSD_GROUPS * SSD_STATE
FOX_HEADS = 16
FOX_HEAD_DIM = 64
FOX_INNER = FOX_HEADS * FOX_HEAD_DIM
FOX_SCALE = FOX_HEAD_DIM ** -0.5
NEG_INF = -1e30

LANES = 128
SUBLANES = 8
V7X_VMEM_BYTES = 64 * 1024 * 1024

FF_CHUNK = D_FF
LOG2E = 1.4426950408889634
Q_SCALE = FOX_SCALE * LOG2E
HIST = SUBLANES


def _dot(a, b):
    return jnp.dot(a, b, preferred_element_type=F32)


def _dot_nt(a, b):
    return lax.dot_general(a, b, (((1,), (1,)), ((), ())), preferred_element_type=F32)


def _split3(x):
    hi = x.astype(BF16)
    r = x - hi.astype(F32)
    mid = r.astype(BF16)
    lo = (r - mid.astype(F32)).astype(BF16)
    return hi, mid, lo


def _dot_exact(a01, x):
    hi, mid, lo = _split3(x)
    return _dot(a01, hi) + _dot(a01, mid) + _dot(a01, lo)


def _dot_exact_r(x, b01):
    hi, mid, lo = _split3(x)
    return _dot(hi, b01) + _dot(mid, b01) + _dot(lo, b01)


def _transpose_exact(x, eye):
    hi, mid, lo = _split3(x)
    return _dot_nt(eye, hi) + _dot_nt(eye, mid) + _dot_nt(eye, lo)


def _tril(n):
    return (lax.broadcasted_iota(jnp.int32, (n, n), 1) <= lax.broadcasted_iota(jnp.int32, (n, n), 0)).astype(BF16)


def _triu(n):
    return (lax.broadcasted_iota(jnp.int32, (n, n), 0) <= lax.broadcasted_iota(jnp.int32, (n, n), 1)).astype(BF16)


def _rms(x):
    return x * lax.rsqrt(jnp.mean(x * x, axis=-1, keepdims=True) + NORM_EPS)


def _silu(x):
    return x * jax.nn.sigmoid(x)


def _params(n_axes, vmem_mib):
    return pltpu.CompilerParams(dimension_semantics=("arbitrary",) * n_axes,
                                vmem_limit_bytes=min(vmem_mib * 1024 * 1024, V7X_VMEM_BYTES - 4 * 1024 * 1024))


def _resident(shape):
    nd = len(shape)
    return pl.BlockSpec(shape, lambda *_: (0,) * nd, pipeline_mode=pl.Buffered(1))


def _ffn_body(*refs, merge, final_norm):
    refs = list(refs)
    x_ref = refs.pop(0)
    if merge:
        os_ref, oa_ref, gs_ref, gf_ref, wf_ref, wo_ref = refs[:6]
        refs = refs[6:]
    nw_ref, wg_ref, wu_ref, wd_ref = refs[:4]
    refs = refs[4:]
    if final_norm:
        fnw_ref = refs.pop(0)
    o_ref, hn_ref = refs

    x = x_ref[...]
    if merge:
        o_fox = _dot(oa_ref[...], wf_ref[...])
        mix = jax.nn.sigmoid(gs_ref[...]) * os_ref[...] + jax.nn.sigmoid(gf_ref[...]) * o_fox
        x = x + _dot(mix.astype(BF16), wo_ref[...])
    hn_ref[...] = (_rms(x) * nw_ref[...]).astype(BF16)
    y = x
    for c in range(D_FF // FF_CHUNK):
        cols = slice(c * FF_CHUNK, (c + 1) * FF_CHUNK)
        g = _dot(hn_ref[...], wg_ref[:, cols])
        u = _dot(hn_ref[...], wu_ref[:, cols])
        y = y + 0.5 * _dot((_silu(g) * u).astype(BF16), wd_ref[cols, :])
    if final_norm:
        y = _rms(y) * fnw_ref[...]
    o_ref[...] = y


def _ffn(x, n_rows, tm, norm_w, wg, wu, wd, merge_args=None, final_norm_w=None, name="ffn"):
    merge = merge_args is not None
    row_spec = pl.BlockSpec((tm, D_MODEL), lambda i: (i, 0))
    args, specs = [x], [row_spec]
    if merge:
        o_ssd, o_att, g_ssd, g_fox, w_fox, w_o = merge_args
        args += [o_ssd, o_att, g_ssd, g_fox, w_fox, w_o]
        specs += [row_spec] * 4 + [_resident(w_fox.shape), _resident(w_o.shape)]
    args += [norm_w, wg, wu, wd]
    specs += [_resident(norm_w.shape), _resident(wg.shape), _resident(wu.shape), _resident(wd.shape)]
    if final_norm_w is not None:
        args.append(final_norm_w)
        specs.append(_resident(final_norm_w.shape))
    return pl.pallas_call(
        functools.partial(_ffn_body, merge=merge, final_norm=final_norm_w is not None),
        grid=(n_rows // tm,),
        in_specs=specs,
        out_specs=row_spec,
        out_shape=jax.ShapeDtypeStruct((n_rows, D_MODEL), F32),
        scratch_shapes=[pltpu.VMEM((tm, D_MODEL), BF16)],
        compiler_params=_params(1, 60),
        name=name,
    )(*args)


_MAIN_WIDTHS = (SSD_INNER, CONV_DIM, FOX_INNER, D_MODEL, D_MODEL)
_Q_INDEX = 2


def _inproj_body(x_ref, nw_ref, wm_ref, wkv_ref, ws_ref, bias_ref, *refs):
    n_main = len(_MAIN_WIDTHS)
    outs, (kt_ref, vt_ref, sm_ref, hn_ref) = refs[:n_main], refs[n_main:]
    hn_ref[...] = (_rms(x_ref[...]) * nw_ref[...]).astype(BF16)
    off = 0
    for i, (ref, w) in enumerate(zip(outs, _MAIN_WIDTHS)):
        val = _dot(hn_ref[...], wm_ref[:, off:off + w])
        ref[...] = (val * Q_SCALE).astype(BF16) if i == _Q_INDEX else val
        off += w
    kt_ref[...] = _dot_nt(wkv_ref[0:FOX_INNER, :], hn_ref[...])
    vt_ref[...] = _dot_nt(wkv_ref[FOX_INNER:, :], hn_ref[...])
    raw = _dot(hn_ref[...], ws_ref[...]) + bias_ref[...]
    t = jnp.log1p(jnp.exp(-jnp.abs(raw)))
    lane = lax.broadcasted_iota(jnp.int32, raw.shape, 1)
    sm_ref[...] = jnp.where(lane < SSD_HEADS, jnp.maximum(raw, 0.0) + t,
                            jnp.where(lane < SSD_HEADS + FOX_HEADS, jnp.minimum(raw, 0.0) - t, 0.0))


def _inproj(x, tm, seq_rows, norm_w, w_main, w_kv_t, w_small, bias_small, name):
    rows = x.shape[0]
    nseq, per_seq = rows // seq_rows, seq_rows // tm
    row = lambda w: pl.BlockSpec((tm, w), lambda i: (i, 0))
    row_shape = lambda i, w: jax.ShapeDtypeStruct((rows, w), BF16 if i == _Q_INDEX else F32)
    t_spec = pl.BlockSpec((None, FOX_INNER, tm), lambda i: (i // per_seq, 0, i % per_seq))
    t_shape = jax.ShapeDtypeStruct((nseq, FOX_INNER, seq_rows), F32)
    return pl.pallas_call(
        _inproj_body,
        grid=(rows // tm,),
        in_specs=[row(D_MODEL), _resident(norm_w.shape), _resident(w_main.shape), _resident(w_kv_t.shape),
                  _resident(w_small.shape), _resident(bias_small.shape)],
        out_specs=[row(w) for w in _MAIN_WIDTHS] + [t_spec, t_spec, row(LANES)],
        out_shape=[row_shape(i, w) for i, w in enumerate(_MAIN_WIDTHS)] + [t_shape, t_shape, row_shape(-1, LANES)],
        scratch_shapes=[pltpu.VMEM((tm, D_MODEL), BF16)],
        compiler_params=_params(1, 60),
        name=name,
    )(x, norm_w, w_main, w_kv_t, w_small, bias_small)


def _cumsum_rows(x):
    return _dot_exact(_tril(x.shape[0]), x)


def _logf_cumsum_body(lt_ref, lm_ref, ct_ref, cm_ref, carry_ref):
    @pl.when(pl.program_id(1) == 0)
    def _():
        row = lax.broadcasted_iota(jnp.int32, (BLOCK, LANES), 0)
        cm = _cumsum_rows(jnp.where(row >= PAD, lm_ref[...], 0.0))
        cm_ref[...] = cm
        carry_ref[...] = cm[BLOCK - 1:BLOCK, :]

    c = _cumsum_rows(lt_ref[...]) + carry_ref[...]
    ct_ref[...] = c
    carry_ref[...] = c[c.shape[0] - 1:, :]


def _logf_cumsum(sm_tok, sm_small, meta_blk, batch, seq, tb):
    nt = seq // tb
    return pl.pallas_call(
        _logf_cumsum_body,
        grid=(batch, nt),
        in_specs=[pl.BlockSpec((tb, LANES), lambda b, j: (b * nt + j, 0)),
                  pl.BlockSpec((BLOCK, LANES), lambda b, j: (meta_blk, 0))],
        out_specs=[pl.BlockSpec((tb, LANES), lambda b, j: (b * nt + j, 0)),
                   pl.BlockSpec((None, BLOCK, LANES), lambda b, j: (b, 0, 0))],
        out_shape=[jax.ShapeDtypeStruct((batch * seq, LANES), F32),
                   jax.ShapeDtypeStruct((batch, BLOCK, LANES), F32)],
        scratch_shapes=[pltpu.VMEM((1, LANES), F32)],
        compiler_params=_params(2, 32),
        name="logf_cumsum",
    )(sm_tok, sm_small)


_BIAS_PIECES = 3
_Q_GROUP = 512


def _fox_prompt_body(q_ref, kt_ref, vt_ref, ct_ref, km_ref, vm_ref, cm_ref, o_ref,
                     k0_sc, k1_sc, v0_sc, v1_sc, km0_sc, km1_sc, vm0_sc, vm1_sc, sa_sc, sb_sc, *, tq):
    hp = pl.program_id(1)
    qi = pl.program_id(2)
    lane = lax.broadcasted_iota(jnp.int32, (1, LANES), 1)
    is0 = lane < FOX_HEAD_DIM
    extra0, extra1 = FOX_HEAD_DIM, 0

    @pl.when(qi == 0)
    def _():
        src = lax.broadcasted_iota(jnp.int32, (_BIAS_PIECES * LANES, LANES), 0)
        dst = lax.broadcasted_iota(jnp.int32, (_BIAS_PIECES * LANES, LANES), 1)
        c_lane = SSD_HEADS + 2 * hp

        def route(head, base):
            return jnp.where((src % LANES == c_lane + head) & (dst == base + src // LANES), 1.0, 0.0).astype(BF16)

        route0, route1 = route(0, extra0), route(1, extra1)
        sub = lax.broadcasted_iota(jnp.int32, (LANES, 1), 0)
        top_rows = sub < FOX_HEAD_DIM
        one_row0 = jnp.where(sub == extra0, 1.0, 0.0)
        one_row1 = jnp.where(sub == extra1, 1.0, 0.0)

        def eye_of(n):
            return (lax.broadcasted_iota(jnp.int32, (n, n), 0) == lax.broadcasted_iota(jnp.int32, (n, n), 1)).astype(BF16)

        def prep(kt, vt, bias, eye):
            k = _dot_nt(eye, kt.astype(BF16))
            pieces = jnp.concatenate(_split3(bias), axis=1)
            k0 = (jnp.where(is0, k, 0.0) + _dot(pieces, route0)).astype(BF16)
            k1 = (jnp.where(is0, 0.0, k) + _dot(pieces, route1)).astype(BF16)
            v0 = jnp.where(top_rows, vt, one_row0).astype(BF16)
            v1 = jnp.where(top_rows, one_row1, vt).astype(BF16)
            return k0, k1, v0, v1

        row = lax.broadcasted_iota(jnp.int32, (BLOCK, LANES), 0)
        bias_m = jnp.where(row >= PAD, cm_ref[...] * (-LOG2E), NEG_INF)
        km0_sc[...], km1_sc[...], vm0_sc[...], vm1_sc[...] = prep(km_ref[...], vm_ref[...], bias_m, eye_of(BLOCK))
        eye_q = eye_of(tq)
        for i in range(k0_sc.shape[0]):
            blk = slice(i * tq, (i + 1) * tq)
            k0_sc[i], k1_sc[i], v0_sc[i], v1_sc[i] = prep(kt_ref[:, blk], vt_ref[:, blk],
                                                         ct_ref[blk, :] * (-LOG2E), eye_q)

    q = q_ref[...].astype(F32)
    ones0 = jnp.where((lane >= extra0) & (lane < extra0 + _BIAS_PIECES), 1.0, 0.0)
    ones1 = jnp.where((lane >= extra1) & (lane < extra1 + _BIAS_PIECES), 1.0, 0.0)
    q0 = jnp.where(is0, q, ones0).astype(BF16)
    q1 = jnp.where(is0, ones1, q).astype(BF16)

    def consume(m, acc, s, vt, mask):
        if mask is not None:
            s = jnp.where(mask, s, NEG_INF)
        m_new = jnp.maximum(m, jnp.max(s, axis=0, keepdims=True))
        p = jnp.exp2(s - m_new).astype(BF16)
        return m_new, acc * jnp.exp2(m - m_new) + _dot(vt, p)

    def consume_last(m, acc, s, vt, qh, km_ref_, vm_ref_, mask):
        s = jnp.where(mask, s, NEG_INF)
        s_meta = _dot_nt(km_ref_[PAD:, :], qh)
        m_new = jnp.maximum(m, jnp.maximum(jnp.max(s, axis=0, keepdims=True), jnp.max(s_meta, axis=0, keepdims=True)))
        p = jnp.exp2(s - m_new).astype(BF16)
        p_meta = jnp.concatenate([jnp.zeros((PAD, tq), BF16), jnp.exp2(s_meta - m_new).astype(BF16)], axis=0)
        return m_new, acc * jnp.exp2(m - m_new) + _dot(vt, p) + _dot(vm_ref_[...], p_meta)

    neg = jnp.full((1, tq), NEG_INF, F32)
    zero = jnp.zeros((LANES, tq), F32)

    def issue(j, s_ref):
        s_ref[0] = _dot_nt(k0_sc[j], q0)
        s_ref[1] = _dot_nt(k1_sc[j], q1)

    def retire(j, s_ref, carry, mask=None):
        m0, acc0, m1, acc1 = carry
        m0, acc0 = consume(m0, acc0, s_ref[0], v0_sc[j], mask)
        m1, acc1 = consume(m1, acc1, s_ref[1], v1_sc[j], mask)
        return m0, acc0, m1, acc1

    def pair(i, carry):
        issue(2 * i + 1, sb_sc)
        carry = retire(2 * i, sa_sc, carry)
        issue(2 * i + 2, sa_sc)
        return retire(2 * i + 1, sb_sc, carry)

    issue(0, sa_sc)
    carry = lax.fori_loop(0, qi // 2, pair, (neg, zero, neg, zero))
    causal = lax.broadcasted_iota(jnp.int32, (tq, 1), 0) <= lax.broadcasted_iota(jnp.int32, (1, tq), 1)
    top = lax.broadcasted_iota(jnp.int32, (LANES, 1), 0) < FOX_HEAD_DIM

    def finish(s_ref, carry):
        m0, acc0, m1, acc1 = carry
        _, acc0 = consume_last(m0, acc0, s_ref[0], v0_sc[qi], q0, km0_sc, vm0_sc, causal)
        _, acc1 = consume_last(m1, acc1, s_ref[1], v1_sc[qi], q1, km1_sc, vm1_sc, causal)
        o_t = jnp.where(top, acc0 / acc0[extra0:extra0 + 1, :], acc1 / acc1[extra1:extra1 + 1, :])
        o_ref[...] = o_t.T.astype(BF16)

    @pl.when(qi % 2 == 0)
    def _():
        finish(sa_sc, carry)

    @pl.when(qi % 2 == 1)
    def _():
        issue(qi, sb_sc)
        finish(sb_sc, retire(qi - 1, sa_sc, carry))


def _fox_prompt(q, k_t, v_t, c_tok, k_small_t, v_small_t, c_meta, meta_blk, batch, seq, tq):
    nq = seq // tq
    npair = FOX_INNER // LANES
    kv_spec = pl.BlockSpec((None, LANES, seq), lambda b, hp, i: (b, hp, 0))
    meta_spec = pl.BlockSpec((None, LANES, BLOCK), lambda b, hp, i: (0, hp, meta_blk))
    blk_spec = pl.BlockSpec((tq, LANES), lambda b, hp, i: (b * nq + i, hp))
    return pl.pallas_call(
        functools.partial(_fox_prompt_body, tq=tq),
        grid=(batch, npair, nq),
        in_specs=[blk_spec, kv_spec, kv_spec, pl.BlockSpec((seq, LANES), lambda b, hp, i: (b, 0)),
                  meta_spec, meta_spec, pl.BlockSpec((None, BLOCK, LANES), lambda b, hp, i: (b, 0, 0))],
        out_specs=blk_spec,
        out_shape=jax.ShapeDtypeStruct((batch * seq, FOX_INNER), BF16),
        scratch_shapes=[pltpu.VMEM((nq, tq, LANES), BF16)] * 2 + [pltpu.VMEM((nq, LANES, tq), BF16)] * 2
                       + [pltpu.VMEM((BLOCK, LANES), BF16)] * 4 + [pltpu.VMEM((2, tq, tq), F32)] * 2,
        compiler_params=_params(3, 48),
        name="fox_prompt",
    )(q, k_t, v_t, c_tok, k_small_t, v_small_t, c_meta)


def _ssd_chunk(xbuf_ref, dt, z, st_ref, ybuf_ref, w_refs):
    L = BLOCK
    cw_ref, cb_ref, a_ref, dskip_ref, nw_ref, wout_ref, tril_ref, eye_ref, e_head_ref = w_refs
    eye = eye_ref[...]
    conv = cb_ref[...]
    for w in range(CONV_WIDTH):
        conv = conv + xbuf_ref[HIST - (CONV_WIDTH - 1) + w:HIST - (CONV_WIDTH - 1) + w + L, :] * cw_ref[w:w + 1, :]
    xc = _silu(conv)
    xs = xc[:, :SSD_INNER]
    gn = SSD_GROUPS * SSD_STATE

    a_col = _dot_exact(tril_ref[...], dt * a_ref[...])
    a_row = _transpose_exact(a_col, eye)
    dt_row = _transpose_exact(dt, eye)
    e_head = e_head_ref[...]
    decay_in = _dot_exact_r(jnp.exp(a_col), e_head)
    to_end = _dot_exact_r(jnp.exp(a_col[L - 1:L, :] - a_col) * dt, e_head)
    xw = (xs * to_end).astype(BF16)
    chunk_decay = decay_in[L - 1:L, :]

    causal = lax.broadcasted_iota(jnp.int32, (L, L), 1) <= lax.broadcasted_iota(jnp.int32, (L, L), 0)
    is0 = lax.broadcasted_iota(jnp.int32, (1, LANES), 1) < SSD_HEAD_DIM
    heads_per_group = SSD_HEADS // SSD_GROUPS
    for g in range(SSD_GROUPS):
        b_g = xc[:, SSD_INNER + g * SSD_STATE:SSD_INNER + (g + 1) * SSD_STATE].astype(BF16)
        c_g = xc[:, SSD_INNER + gn + g * SSD_STATE:SSD_INNER + gn + (g + 1) * SSD_STATE].astype(BF16)
        cb = _dot_nt(c_g, b_g)
        lo, hi = g * GROUP_W, (g + 1) * GROUP_W
        h_in = st_ref[g]
        y_off = _dot(c_g, h_in.astype(BF16)) * decay_in[:, lo:hi]
        ybuf_ref[:, lo:hi] = y_off
        b_t = _dot_nt(eye, b_g).astype(BF16)
        st_ref[g] = h_in * chunk_decay[:, lo:hi] + _dot(b_t, xw[:, lo:hi])
        for pair in range(heads_per_group // 2):
            h0 = g * heads_per_group + 2 * pair
            c0 = h0 * SSD_HEAD_DIM
            x_pair = xs[:, c0:c0 + LANES]
            y_pair = jnp.zeros((L, LANES), F32)
            for k, xm in ((0, jnp.where(is0, x_pair, 0.0)), (1, jnp.where(is0, 0.0, x_pair))):
                h = h0 + k
                seg = a_col[:, h:h + 1] - a_row[h:h + 1, :]
                wgt = cb * jnp.exp(jnp.where(causal, seg, -jnp.inf)) * dt_row[h:h + 1, :]
                y_pair = y_pair + _dot(wgt.astype(BF16), xm.astype(BF16))
            ybuf_ref[:, c0:c0 + LANES] += y_pair

    y = (ybuf_ref[...] + xs * dskip_ref[...]) * _silu(z)
    yn = (_rms(y) * nw_ref[...]).astype(BF16)
    return _dot(yn, wout_ref[...])


_SSD_WEIGHT_SHAPES = ((SUBLANES, CONV_DIM), (1, CONV_DIM), (1, LANES), (1, SSD_INNER), (1, SSD_INNER),
                      (SSD_INNER, D_MODEL),
                      (BLOCK, BLOCK), (BLOCK, BLOCK), (LANES, SSD_INNER))


def _ssd_routing_matrices():
    r = jnp.arange(LANES)[:, None]
    tril = (jnp.arange(BLOCK)[None, :] <= jnp.arange(BLOCK)[:, None]).astype(BF16)
    eye = jnp.eye(BLOCK, dtype=BF16)
    e_head = (jnp.arange(SSD_INNER)[None, :] // SSD_HEAD_DIM == r).astype(BF16)
    return tril, eye, e_head


def _ssd_prompt_body(xt_ref, zt_ref, st_ref_in, xm_ref, sm_ref, *rest):
    w_refs = rest[:len(_SSD_WEIGHT_SHAPES)]
    o_ref, state_ref, xbuf_ref, st_ref, ybuf_ref = rest[len(_SSD_WEIGHT_SHAPES):]
    c = pl.program_id(1)
    lane = lax.broadcasted_iota(jnp.int32, (BLOCK, LANES), 1)
    row = lax.broadcasted_iota(jnp.int32, (BLOCK, LANES), 0)

    @pl.when(c == 0)
    def _():
        st_ref[...] = jnp.zeros_like(st_ref)
        xbuf_ref[0:HIST, :] = jnp.zeros((HIST, CONV_DIM), F32)
        xbuf_ref[HIST:, :] = xm_ref[...]

    @pl.when(c > 0)
    def _():
        xbuf_ref[HIST:, :] = xt_ref[...]

    dt = jnp.where(c == 0,
                   jnp.where((lane < SSD_HEADS) & (row >= PAD), sm_ref[...], 0.0),
                   jnp.where(lane < SSD_HEADS, st_ref_in[...], 0.0))
    o_ref[...] = _ssd_chunk(xbuf_ref, dt, zt_ref[...], st_ref, ybuf_ref, w_refs)
    xbuf_ref[0:HIST, :] = xbuf_ref[BLOCK:BLOCK + HIST, :]

    @pl.when(c == pl.num_programs(1) - 1)
    def _():
        state_ref[...] = st_ref[...]


def _ssd_prompt(xbc, z, sm, xbc_small, sm_small, meta_blk, weights, batch, seq):
    nc = seq // BLOCK
    tok = lambda w: pl.BlockSpec((BLOCK, w), lambda b, c: (b * nc + jnp.maximum(c - 1, 0), 0))
    meta = lambda w: pl.BlockSpec((BLOCK, w), lambda b, c: (meta_blk, 0))
    return pl.pallas_call(
        _ssd_prompt_body,
        grid=(batch, nc + 1),
        in_specs=[tok(CONV_DIM), tok(SSD_INNER), tok(LANES), meta(CONV_DIM), meta(LANES)]
                 + [_resident(s) for s in _SSD_WEIGHT_SHAPES],
        out_specs=[tok(D_MODEL),
                   pl.BlockSpec((None, SSD_GROUPS, SSD_STATE, GROUP_W), lambda b, c: (b, 0, 0, 0))],
        out_shape=[jax.ShapeDtypeStruct((batch * seq, D_MODEL), F32),
                   jax.ShapeDtypeStruct((batch, SSD_GROUPS, SSD_STATE, GROUP_W), F32)],
        scratch_shapes=[pltpu.VMEM((HIST + BLOCK, CONV_DIM), F32),
                        pltpu.VMEM((SSD_GROUPS, SSD_STATE, GROUP_W), F32),
                        pltpu.VMEM((BLOCK, SSD_INNER), F32)],
        compiler_params=_params(2, 40),
        name="ssd_prompt",
    )(xbc, z, sm, xbc_small, sm_small, *weights)


def _ssd_sample_body(x_ref, z_ref, sm_ref, hist_ref, state0_ref, *rest, n_new):
    w_refs = rest[:len(_SSD_WEIGHT_SHAPES)]
    o_ref, state_ref, xbuf_ref, st_ref, ybuf_ref, dt_ref, zbuf_ref = rest[len(_SSD_WEIGHT_SHAPES):]
    rows = x_ref.shape[0]
    xbuf_ref[0:HIST, :] = hist_ref[...]
    xbuf_ref[HIST:HIST + rows, :] = x_ref[...]
    xbuf_ref[HIST + rows:, :] = jnp.zeros((BLOCK - rows, CONV_DIM), F32)
    lane = lax.broadcasted_iota(jnp.int32, (rows, LANES), 1)
    row = lax.broadcasted_iota(jnp.int32, (rows, LANES), 0)
    dt_ref[...] = jnp.zeros_like(dt_ref)
    dt_ref[0:rows, :] = jnp.where((lane < SSD_HEADS) & (row < n_new), sm_ref[...], 0.0)
    zbuf_ref[...] = jnp.zeros_like(zbuf_ref)
    zbuf_ref[0:rows, :] = z_ref[...]
    st_ref[...] = state0_ref[...]
    out = _ssd_chunk(xbuf_ref, dt_ref[...], zbuf_ref[...], st_ref, ybuf_ref, w_refs)
    o_ref[...] = out[0:rows, :]
    state_ref[...] = st_ref[...]


def _ssd_sample(x8, z8, sm8, hist8, state0, weights, n_new):
    nb, rows = x8.shape[0], x8.shape[1]
    per = lambda w: pl.BlockSpec((None, rows, w), lambda b: (b, 0, 0))
    st_spec = pl.BlockSpec((None, SSD_GROUPS, SSD_STATE, GROUP_W), lambda b: (b, 0, 0, 0))
    return pl.pallas_call(
        functools.partial(_ssd_sample_body, n_new=n_new),
        grid=(nb,),
        in_specs=[per(CONV_DIM), per(SSD_INNER), per(LANES), per(CONV_DIM), st_spec]
                 + [_resident(s) for s in _SSD_WEIGHT_SHAPES],
        out_specs=[per(D_MODEL), st_spec],
        out_shape=[jax.ShapeDtypeStruct((nb, rows, D_MODEL), F32),
                   jax.ShapeDtypeStruct((nb, SSD_GROUPS, SSD_STATE, GROUP_W), F32)],
        scratch_shapes=[pltpu.VMEM((HIST + BLOCK, CONV_DIM), F32),
                        pltpu.VMEM((SSD_GROUPS, SSD_STATE, GROUP_W), F32),
                        pltpu.VMEM((BLOCK, SSD_INNER), F32),
                        pltpu.VMEM((BLOCK, LANES), F32),
                        pltpu.VMEM((BLOCK, SSD_INNER), F32)],
        compiler_params=_params(1, 40),
        name="ssd_sample",
    )(x8, z8, sm8, hist8, state0, *weights)


def _page_cumsum_body(x_ref, c_ref, tot_ref):
    both = jnp.concatenate([_triu(LANES), jnp.ones((LANES, LANES), BF16)], axis=1)
    r = _dot_exact_r(x_ref[...], both)
    c_ref[...] = r[:, :LANES]
    tot_ref[...] = r[:, LANES:]


def _page_cumsum(lf_t, tr):
    rows = lf_t.shape[0]
    spec = pl.BlockSpec((tr, LANES), lambda i: (i, 0))
    return pl.pallas_call(
        _page_cumsum_body,
        grid=(rows // tr,),
        in_specs=[spec],
        out_specs=[spec, spec],
        out_shape=[jax.ShapeDtypeStruct((rows, LANES), F32)] * 2,
        compiler_params=_params(1, 32),
        name="page_cumsum",
    )(lf_t)


def _fox_sample_body(pt_ref, q_ref, kn_ref, vn_ref, ln_ref, *refs, pps, n_new):
    k_refs, v_refs, c_refs, t_refs = refs[:pps], refs[pps:2 * pps], refs[2 * pps:3 * pps], refs[3 * pps:4 * pps]
    o_ref, qbd_sc, m_sc, l_sc, acc_sc, carry_sc, kn_sc, vn_sc = refs[4 * pps:]
    del pt_ref
    j = pl.program_id(1)
    nrow = n_new * FOX_HEADS

    @pl.when(j == 0)
    def _():
        col_head = lax.broadcasted_iota(jnp.int32, (FOX_HEADS, FOX_INNER), 1) // FOX_HEAD_DIM
        own = col_head == lax.broadcasted_iota(jnp.int32, (FOX_HEADS, FOX_INNER), 0)
        q = q_ref[...].astype(F32)
        for t in range(n_new):
            qbd_sc[t * FOX_HEADS:(t + 1) * FOX_HEADS, :] = jnp.where(own, q[t:t + 1, :], 0.0).astype(BF16)
        m_sc[...] = jnp.full_like(m_sc, NEG_INF)
        l_sc[...] = jnp.zeros_like(l_sc)
        acc_sc[...] = jnp.zeros_like(acc_sc)
        carry_sc[...] = jnp.zeros_like(carry_sc)

    qbd = qbd_sc[...]

    def tile_rows(bias16):
        return jnp.concatenate([bias16] * n_new, axis=0)

    def accumulate(s_list, pv_fn_list):
        m_old = m_sc[...]
        m_new = m_old
        for s in s_list:
            m_new = jnp.maximum(m_new, jnp.max(s, axis=1, keepdims=True))
        alpha = jnp.exp2(m_old - m_new)
        l_new = alpha * l_sc[...]
        pv = jnp.zeros(acc_sc.shape, F32)
        for s, pv_fn in zip(s_list, pv_fn_list):
            p = jnp.exp2(s - m_new)
            l_new = l_new + jnp.sum(p, axis=1, keepdims=True)
            pv = pv + pv_fn(p.astype(BF16))
        m_sc[...] = m_new
        l_sc[...] = l_new
        acc_sc[...] = acc_sc[...] * alpha + pv

    carry = carry_sc[...]
    s_list, pv_fns = [], []
    for i in range(pps):
        kt = k_refs[i][...].reshape(FOX_INNER, BLOCK).astype(BF16)
        vt = v_refs[i][...].reshape(FOX_INNER, BLOCK).astype(BF16)
        s_list.append(_dot(qbd, kt) - tile_rows((carry + c_refs[i][...]) * LOG2E))
        pv_fns.append(functools.partial(lambda p, vt: _dot_nt(p, vt), vt=vt))
        carry = carry + t_refs[i][...]
    carry_sc[...] = carry
    accumulate(s_list, pv_fns)

    @pl.when(j == pl.num_programs(1) - 1)
    def _():
        rows = kn_ref.shape[0]
        kn_sc[...] = jnp.zeros_like(kn_sc)
        vn_sc[...] = jnp.zeros_like(vn_sc)
        kn_sc[0:rows, :] = kn_ref[...].astype(BF16)
        vn_sc[0:rows, :] = vn_ref[...].astype(BF16)
        bias = (carry_sc[...] + _dot_exact_r(ln_ref[...], _triu(LANES))) * LOG2E
        s = _dot_nt(qbd, kn_sc[...]) - tile_rows(bias)
        key = lax.broadcasted_iota(jnp.int32, (nrow, BLOCK), 1)
        tok = lax.broadcasted_iota(jnp.int32, (nrow, BLOCK), 0) // FOX_HEADS
        s = jnp.where(key <= tok, s, NEG_INF)
        vn = vn_sc[...]
        accumulate([s], [lambda p: _dot(p, vn)])
        o = acc_sc[...] / l_sc[...]
        row_head = lax.broadcasted_iota(jnp.int32, (nrow, FOX_INNER), 0) % FOX_HEADS
        col_head = lax.broadcasted_iota(jnp.int32, (nrow, FOX_INNER), 1) // FOX_HEAD_DIM
        o = jnp.where(row_head == col_head, o, 0.0)
        o_ref[...] = jnp.sum(o.reshape(n_new, FOX_HEADS, FOX_INNER), axis=1)


def _fox_sample(page_table, q, k_new8, v_new8, lf_new_t, cache_kt, cache_vt, c_page, tot_page, pps):
    nb, n_new = q.shape[0], q.shape[1]
    n_pages = page_table.shape[1]
    rows8 = k_new8.shape[1]
    pt_flat = page_table.reshape(-1)

    def page_idx(i):
        return lambda b, j, pt: pt[b * n_pages + j * pps + i]

    def kv_spec(i):
        f = page_idx(i)
        return pl.BlockSpec((None, None, FOX_HEADS, FOX_HEAD_DIM, BLOCK), lambda b, j, pt: (0, f(b, j, pt), 0, 0, 0))

    def c_spec(i):
        f = page_idx(i)
        return pl.BlockSpec((None, FOX_HEADS, LANES), lambda b, j, pt: (f(b, j, pt), 0, 0))

    per = lambda r, w: pl.BlockSpec((None, r, w), lambda b, j, pt: (b, 0, 0))
    in_specs = [per(n_new, FOX_INNER), per(rows8, FOX_INNER), per(rows8, FOX_INNER), per(FOX_HEADS, LANES)]
    in_specs += [kv_spec(i) for i in range(pps)] * 2 + [c_spec(i) for i in range(pps)] * 2
    nrow = n_new * FOX_HEADS
    grid_spec = pltpu.PrefetchScalarGridSpec(
        num_scalar_prefetch=1,
        grid=(nb, n_pages // pps),
        in_specs=in_specs,
        out_specs=per(n_new, FOX_INNER),
        scratch_shapes=[pltpu.VMEM((nrow, FOX_INNER), BF16), pltpu.VMEM((nrow, 1), F32), pltpu.VMEM((nrow, 1), F32),
                        pltpu.VMEM((nrow, FOX_INNER), F32), pltpu.VMEM((FOX_HEADS, LANES), F32),
                        pltpu.VMEM((BLOCK, FOX_INNER), BF16), pltpu.VMEM((BLOCK, FOX_INNER), BF16)],
    )
    return pl.pallas_call(
        functools.partial(_fox_sample_body, pps=pps, n_new=n_new),
        grid_spec=grid_spec,
        out_shape=jax.ShapeDtypeStruct((nb, n_new, FOX_INNER), F32),
        compiler_params=_params(2, 56),
        name="fox_sample",
    )(pt_flat, q, k_new8, v_new8, lf_new_t, *([cache_kt] * pps), *([cache_vt] * pps), *([c_page] * pps),
      *([tot_page] * pps))


def _ffn_weights(w_gate, w_up, w_down):
    return w_gate.astype(BF16), w_up.astype(BF16), w_down.astype(BF16)


def _pick_tile(n, candidates):
    for t in candidates:
        if n % t == 0:
            return t
    raise ValueError(f"no tile for {n}")


def kernel(x_prompt, x_sample, cache_k, cache_v, cache_logf, page_table, state_conv, state_ssm, meta_tokens, norm_ffn1, w_ffn1_gate, w_ffn1_up, w_ffn1_down, norm_mix, w_in, conv_w, conv_b, dt_bias, A_log, D_skip, ssd_norm, w_ssd_out, b_forget, w_fox_out, w_o, norm_ffn2, w_ffn2_gate, w_ffn2_up, w_ffn2_down, norm_final):
    batch, seq, _ = x_prompt.shape
    nb, n_new, _ = x_sample.shape
    depth, n_phys, page = cache_k.shape[0], cache_k.shape[1], cache_k.shape[2]
    assert depth == 1 and page == BLOCK and seq % BLOCK == 0 and n_new <= SUBLANES
    n_s = nb * n_new
    n_s_pad = -(-n_s // BLOCK) * BLOCK
    meta_blk = n_s_pad // BLOCK
    rows_p = batch * seq

    ffn1_w = _ffn_weights(w_ffn1_gate[0], w_ffn1_up[0], w_ffn1_down[0])
    ffn2_w = _ffn_weights(w_ffn2_gate[0], w_ffn2_up[0], w_ffn2_down[0])
    wz, wxbc, wdt, wq, wk, wv, wf, wgs, wgf = jnp.split(
        w_in[0], [1024, 2560, 2576, 3600, 4624, 5648, 5664, 6688], axis=1)
    w_main = jnp.concatenate([wz, wxbc, wq, wgs, wgf], axis=1).astype(BF16)
    w_kv_t = jnp.concatenate([wk.T, wv.T], axis=0).astype(BF16)
    w_small = jnp.pad(jnp.concatenate([wdt, wf], axis=1), ((0, 0), (0, LANES - SSD_HEADS - FOX_HEADS))).astype(BF16)
    bias_small = jnp.pad(jnp.concatenate([dt_bias[0], b_forget[0]]), (0, LANES - SSD_HEADS - FOX_HEADS))[None, :]
    row = lambda a: a.reshape(1, -1).astype(F32)
    ssd_w = (jnp.pad(conv_w[0], ((0, SUBLANES - CONV_WIDTH), (0, 0))), row(conv_b[0]),
             jnp.pad(-jnp.exp(A_log[0].astype(F32)), (0, LANES - SSD_HEADS))[None, :],
             row(jnp.repeat(D_skip[0], SSD_HEAD_DIM)), row(ssd_norm[0]), w_ssd_out[0].astype(BF16),
             *_ssd_routing_matrices())
    w_fox_b, w_o_b = w_fox_out[0].astype(BF16), w_o[0].astype(BF16)

    xp = x_prompt.reshape(rows_p, D_MODEL)
    x_small = jnp.concatenate([x_sample.reshape(n_s, D_MODEL), jnp.zeros((n_s_pad - n_s + PAD, D_MODEL), F32),
                               meta_tokens.astype(F32)], axis=0)
    rows_s = n_s_pad + BLOCK

    tm_ffn = _pick_tile(rows_p, (512, 256, 128))
    tm_in = _pick_tile(rows_p, (256, 128))

    x1 = _ffn(xp, rows_p, tm_ffn, row(norm_ffn1[0]), *ffn1_w, name="ffn1")
    tm_small = _pick_tile(rows_s, (256, 128))
    x1s = _ffn(x_small, rows_s, tm_small, row(norm_ffn1[0]), *ffn1_w, name="ffn1_small")
    z, xbc, q, gs, gf, k_t, v_t, sm = _inproj(x1, tm_in, seq, row(norm_mix[0]), w_main, w_kv_t, w_small, bias_small,
                                              "inproj")
    zs, xbcs, qs, gss, gfs, ks_t, vs_t, sms = _inproj(x1s, tm_small, rows_s, row(norm_mix[0]), w_main, w_kv_t, w_small,
                                                      bias_small, "inproj_small")
    ks, vs = ks_t[0].T, vs_t[0].T

    o_ssd, st_p = _ssd_prompt(xbc, z, sm, xbcs, sms, meta_blk, ssd_w, batch, seq)

    tq = _pick_tile(seq, (512, 256, 128))
    c_tok, c_meta = _logf_cumsum(sm, sms, meta_blk, batch, seq, _pick_tile(seq, (512, 256, 128)))
    lo, hi = SSD_HEADS, SSD_HEADS + FOX_HEADS
    o_att = _fox_prompt(q, k_t, v_t, c_tok, ks_t, vs_t, c_meta, meta_blk, batch, seq, tq)

    y_p = _ffn(x1, rows_p, tm_ffn, row(norm_ffn2[0]), *ffn2_w, merge_args=(o_ssd, o_att, gs, gf, w_fox_b, w_o_b),
               final_norm_w=row(norm_final), name="merge_ffn2")

    pad8 = lambda a: jnp.pad(a[:n_s].reshape(nb, n_new, -1), ((0, 0), (0, SUBLANES - n_new), (0, 0)))
    hist8 = jnp.pad(state_conv[0].astype(F32), ((0, 0), (SUBLANES - (CONV_WIDTH - 1), 0), (0, 0)))
    rper = SSD_HEADS // SSD_GROUPS
    state0 = state_ssm[0].astype(F32).reshape(nb, SSD_GROUPS, rper, SSD_HEAD_DIM, SSD_STATE)
    state0 = state0.transpose(0, 1, 4, 2, 3).reshape(nb, SSD_GROUPS, SSD_STATE, GROUP_W)
    o_ssd_s8, st_s = _ssd_sample(pad8(xbcs), pad8(zs), pad8(sms), hist8, state0, ssd_w, n_new)

    lf_t = cache_logf[0].astype(F32).transpose(0, 2, 1).reshape(n_phys * FOX_HEADS, page)
    c_page, tot_page = _page_cumsum(lf_t, _pick_tile(n_phys * FOX_HEADS, (4096, 2048, 1024, 512, 256, 128, 16)))
    c_page = c_page.reshape(n_phys, FOX_HEADS, page)
    tot_page = tot_page.reshape(n_phys, FOX_HEADS, page)
    lf_new_t = jnp.pad(sms[:n_s, lo:hi].reshape(nb, n_new, FOX_HEADS).transpose(0, 2, 1),
                       ((0, 0), (0, 0), (0, LANES - n_new)))
    pps = _pick_tile(page_table.shape[1], (16, 8, 4, 2, 1))
    o_att_s = _fox_sample(page_table, qs[:n_s].reshape(nb, n_new, FOX_INNER), pad8(ks), pad8(vs), lf_new_t,
                          cache_k.transpose(0, 1, 3, 4, 2), cache_v.transpose(0, 1, 3, 4, 2), c_page, tot_page, pps)

    rows_pad = lambda a: jnp.pad(a.reshape(n_s, -1), ((0, n_s_pad - n_s), (0, 0)))
    y_s = _ffn(x1s, n_s_pad, BLOCK, row(norm_ffn2[0]), *ffn2_w,
               merge_args=(rows_pad(o_ssd_s8[:, :n_new]), rows_pad(o_att_s).astype(BF16), gss, gfs, w_fox_b, w_o_b),
               final_norm_w=row(norm_final), name="merge_ffn2_small")

    meta_rows = slice(n_s_pad + PAD, n_s_pad + BLOCK)

    def with_meta(tok, small, width):
        m = jnp.broadcast_to(small[meta_rows][None], (batch, N_META, width))
        return jnp.concatenate([m, tok.reshape(batch, seq, width)], axis=1)

    def unpack_state(st, n):
        st = st.reshape(n, SSD_GROUPS, SSD_STATE, rper, SSD_HEAD_DIM).transpose(0, 1, 3, 4, 2)
        return st.reshape(1, n, SSD_HEADS, SSD_HEAD_DIM, SSD_STATE)

    y_prompt = y_p.reshape(batch, seq, D_MODEL)
    y_sample = y_s[:n_s].reshape(nb, n_new, D_MODEL)
    def with_meta_t(tok_t, small_t):
        m = jnp.broadcast_to(small_t[:, :, meta_rows], (batch, FOX_INNER, N_META))
        full = jnp.concatenate([m, tok_t], axis=2).reshape(batch, FOX_HEADS, FOX_HEAD_DIM, seq + N_META)
        return full.transpose(0, 3, 1, 2)[None]

    k_p = with_meta_t(k_t, ks_t)
    v_p = with_meta_t(v_t, vs_t)
    lf_p = with_meta(sm[:, lo:hi], sms[:, lo:hi], FOX_HEADS)[None]
    conv_p = xbc.reshape(batch, seq, CONV_DIM)[:, seq - (CONV_WIDTH - 1):][None]
    ssm_p = unpack_state(st_p, batch)
    k_s = ks[:n_s].reshape(1, nb, n_new, FOX_HEADS, FOX_HEAD_DIM)
    v_s = vs[:n_s].reshape(1, nb, n_new, FOX_HEADS, FOX_HEAD_DIM)
    lf_s = sms[:n_s, lo:hi].reshape(1, nb, n_new, FOX_HEADS)
    conv_s = jnp.concatenate([state_conv[0].astype(F32), xbcs[:n_s].reshape(nb, n_new, CONV_DIM)], axis=1)[:, n_new:][None]
    ssm_s = unpack_state(st_s, nb)
    return (y_prompt, y_sample, k_p, v_p, lf_p, conv_p, ssm_p, k_s, v_s, lf_s, conv_s, ssm_s)
```
